```python
import jax, jax.numpy as jnp
from jax import lax
import numpy as np

D_MODEL = 1024
BATCH = 16
SEQ = 2048
DEPTH = 1

CHUNK = 64
D_MIX = D_MODEL
ATTN_WIDTH = D_MIX // 2
HGRN_WIDTH = D_MIX - ATTN_WIDTH
ATTN_HEAD_DIM = 64
ATTN_HEADS = ATTN_WIDTH // ATTN_HEAD_DIM
HGRN_HEAD_DIM = 128
HGRN_HEADS = HGRN_WIDTH // HGRN_HEAD_DIM
LEFT_CHUNKS = 8
BAND = (LEFT_CHUNKS + 1) * CHUNK
REL_CLIP = 128
N_REL = 2 * REL_CLIP + 1
D_FF = 2816
RMS_EPS = 1e-6
PROJ_SIZES = (ATTN_WIDTH, ATTN_WIDTH, ATTN_WIDTH, HGRN_WIDTH, HGRN_WIDTH, HGRN_WIDTH, HGRN_WIDTH)
PROJ_COLS = sum(PROJ_SIZES)
PROJ_SPLITS = tuple(int(v) for v in np.cumsum(PROJ_SIZES)[:-1])

kernel_name = "hybrid_chunk_attn_hgrn2_macaron"


def rms_norm(x, g):
    xf = x.astype(jnp.float32)
    y = xf * lax.rsqrt(jnp.mean(xf * xf, axis=-1, keepdims=True) + RMS_EPS)
    return (y * g.astype(jnp.float32)).astype(x.dtype)


def swiglu_ffn(h, w_gate, w_up, w_down):
    return (jax.nn.silu(h @ w_gate) * (h @ w_up)) @ w_down


def chunked_band_attention(q, k, v, rel_bias):
    B, S, H, Dh = q.shape
    n_chunks = S // CHUNK
    pad = LEFT_CHUNKS * CHUNK
    k_pad = jnp.pad(k, ((0, 0), (pad, 0), (0, 0), (0, 0)))
    v_pad = jnp.pad(v, ((0, 0), (pad, 0), (0, 0), (0, 0)))
    t_pos = jnp.arange(CHUNK)[:, None] + pad
    s_pos = jnp.arange(BAND)[None, :]
    rel_idx = jnp.clip(t_pos - s_pos, -REL_CLIP, REL_CLIP) + REL_CLIP
    bias = rel_bias.astype(jnp.float32)[:, rel_idx]
    scale = Dh ** -0.5
    q_chunks = q.reshape(B, n_chunks, CHUNK, H, Dh).transpose(1, 0, 2, 3, 4)

    def one_chunk(args):
        c, qc = args
        start = c * CHUNK
        kb = lax.dynamic_slice_in_dim(k_pad, start, BAND, axis=1)
        vb = lax.dynamic_slice_in_dim(v_pad, start, BAND, axis=1)
        scores = jnp.einsum('bqhd,bkhd->bhqk', qc, kb,
                            preferred_element_type=jnp.float32) * scale + bias
        key_pos = start - pad + jnp.arange(BAND)
        scores = jnp.where((key_pos >= 0)[None, None, None, :], scores, -jnp.inf)
        p = jax.nn.softmax(scores, axis=-1)
        return jnp.einsum('bhqk,bkhd->bqhd', p.astype(vb.dtype), vb)

    out = lax.map(one_chunk, (jnp.arange(n_chunks), q_chunks))
    return out.transpose(1, 0, 2, 3, 4).reshape(B, S, H * Dh)


def hgrn2_chunkwise(q, k, v, log_f):
    B, S, H, Dk = q.shape
    Dv = v.shape[-1]
    n_chunks = S // CHUNK

    def to_chunks(a):
        return a.reshape(B, n_chunks, CHUNK, H, a.shape[-1]).transpose(1, 0, 3, 2, 4)

    qc, kc, vc, gc = to_chunks(q), to_chunks(k), to_chunks(v), to_chunks(log_f)
    causal = jnp.tril(jnp.ones((CHUNK, CHUNK), dtype=bool))[None, None, :, :, None]

    def step(state, inp):
        qi, ki, vi, gi = inp
        b = jnp.cumsum(gi, axis=2)
        diff = b[:, :, :, None, :] - b[:, :, None, :, :]
        decay = jnp.exp(jnp.where(causal, diff, -jnp.inf))
        scores = jnp.einsum('bhtk,bhsk,bhtsk->bhts', qi, ki, decay)
        o = (jnp.einsum('bhts,bhsv->bhtv', scores, vi)
             + jnp.einsum('bhtk,bhkv->bhtv', qi * jnp.exp(b), state))
        b_last = b[:, :, -1:, :]
        new_state = (state * jnp.exp(b_last[:, :, 0, :])[..., None]
                     + jnp.einsum('bhsk,bhsv->bhkv', ki * jnp.exp(b_last - b), vi))
        return new_state, o

    s0 = jnp.zeros((B, H, Dk, Dv), jnp.float32)
    _, o = lax.scan(step, s0, (qc, kc, vc, gc))
    return o.transpose(1, 0, 3, 2, 4).reshape(B, S, H, Dv)


def setup_inputs(seed: int = 0) -> dict:
    key = jax.random.key(seed)
    ks = jax.random.split(key, 20)
    f32 = jnp.float32

    def w(k, shape, fan_in):
        return jax.random.normal(k, shape, f32) * (fan_in ** -0.5)

    def gain(k, shape):
        return 1.0 + 0.05 * jax.random.normal(k, shape, f32)

    return {
        "x": jax.random.normal(ks[0], (BATCH, SEQ, D_MODEL), f32),
        "ffn1_norm_g": gain(ks[1], (DEPTH, D_MODEL)),
        "ffn1_w_gate": w(ks[2], (DEPTH, D_MODEL, D_FF), D_MODEL),
        "ffn1_w_up": w(ks[3], (DEPTH, D_MODEL, D_FF), D_MODEL),
        "ffn1_w_down": w(ks[4], (DEPTH, D_FF, D_MODEL), D_FF),
        "mix_norm_g": gain(ks[5], (DEPTH, D_MODEL)),
        "w_in": w(ks[6], (DEPTH, D_MODEL, PROJ_COLS), D_MODEL),
        "attn_q_norm_g": gain(ks[7], (DEPTH, ATTN_HEAD_DIM)),
        "attn_k_norm_g": gain(ks[8], (DEPTH, ATTN_HEAD_DIM)),
        "attn_rel_bias": 0.1 * jax.random.normal(ks[9], (DEPTH, ATTN_HEADS, N_REL), f32),
        "hgrn_lower_bounds": 0.1 * jax.random.normal(ks[10], (DEPTH + 1, HGRN_WIDTH), f32),
        "hgrn_out_norm_g": gain(ks[11], (DEPTH, HGRN_HEAD_DIM)),
        "w_out": w(ks[12], (DEPTH, D_MIX, D_MODEL), D_MIX),
        "ffn2_norm_g": gain(ks[13], (DEPTH, D_MODEL)),
        "ffn2_w_gate": w(ks[14], (DEPTH, D_MODEL, D_FF), D_MODEL),
        "ffn2_w_up": w(ks[15], (DEPTH, D_MODEL, D_FF), D_MODEL),
        "ffn2_w_down": w(ks[16], (DEPTH, D_FF, D_MODEL), D_FF),
    }


def reference(x, ffn1_norm_g, ffn1_w_gate, ffn1_w_up, ffn1_w_down, mix_norm_g, w_in,
              attn_q_norm_g, attn_k_norm_g, attn_rel_bias, hgrn_lower_bounds, hgrn_out_norm_g,
              w_out, ffn2_norm_g, ffn2_w_gate, ffn2_w_up, ffn2_w_down):
    B, S, _ = x.shape
    lb_all = jnp.cumsum(jax.nn.softmax(hgrn_lower_bounds.astype(jnp.float32), axis=0), axis=0)

    for l in range(DEPTH):
        h = rms_norm(x, ffn1_norm_g[l])
        x = x + 0.5 * swiglu_ffn(h, ffn1_w_gate[l], ffn1_w_up[l], ffn1_w_down[l])

        h = rms_norm(x, mix_norm_g[l])
        proj = h @ w_in[l]
        aq, ak, av, hq, hf, hi, hg = jnp.split(proj, PROJ_SPLITS, axis=-1)

        aq = rms_norm(aq.reshape(B, S, ATTN_HEADS, ATTN_HEAD_DIM), attn_q_norm_g[l])
        ak = rms_norm(ak.reshape(B, S, ATTN_HEADS, ATTN_HEAD_DIM), attn_k_norm_g[l])
        av = av.reshape(B, S, ATTN_HEADS, ATTN_HEAD_DIM)
        attn_out = chunked_band_attention(aq, ak, av, attn_rel_bias[l])

        lb = lb_all[l]
        f = lb + (1.0 - lb) * jax.nn.sigmoid(hf.astype(jnp.float32))
        shp = (B, S, HGRN_HEADS, HGRN_HEAD_DIM)
        rq = jax.nn.silu(hq.astype(jnp.float32)).reshape(shp)
        rk = (1.0 - f).reshape(shp)
        rv = hi.astype(jnp.float32).reshape(shp)
        ro = hgrn2_chunkwise(rq, rk, rv, jnp.log(f).reshape(shp))
        ro = rms_norm(ro, hgrn_out_norm_g[l]) * jax.nn.silu(hg.astype(jnp.float32).reshape(shp))
        hgrn_out = ro.reshape(B, S, HGRN_WIDTH).astype(x.dtype)

        x = x + jnp.concatenate([attn_out, hgrn_out], axis=-1) @ w_out[l]

        h = rms_norm(x, ffn2_norm_g[l])
        x = x + 0.5 * swiglu_ffn(h, ffn2_w_gate[l], ffn2_w_up[l], ffn2_w_down[l])
    return x
```

```python
import functools

import numpy as np
import jax
import jax.numpy as jnp
from jax import lax
from jax.experimental import pallas as pl
from jax.experimental.pallas import tpu as pltpu

D_MODEL = 1024
CHUNK = 64
ATTN_WIDTH = 512
HGRN_WIDTH = 512
ATTN_HEAD_DIM = 64
ATTN_HEADS = 8
HGRN_HEAD_DIM = 128
HGRN_HEADS = 4
LEFT_CHUNKS = 8
BAND = (LEFT_CHUNKS + 1) * CHUNK
PAD = LEFT_CHUNKS * CHUNK
REL_CLIP = 128
D_FF = 2816
RMS_EPS = 1e-6
PROJ_COLS = 3 * ATTN_WIDTH + 4 * HGRN_WIDTH

LANES = 128
MXU_TILE = 256
FF_TILE = MXU_TILE
ROW_TILE = 512
ATTN_Q_CHUNKS = 2
HGRN_CHUNKS = 8
N_LEVELS = 6
VMEM_LIMIT = 56 * 1024 * 1024

BF16 = jnp.bfloat16
F32 = jnp.float32


def _dot(a, b):
    return jnp.dot(a, b, preferred_element_type=F32)


def _dot_nt(a, b):
    return lax.dot_general(a, b, (((1,), (1,)), ((), ())), preferred_element_type=F32)


def _dot_tn(a, b):
    return lax.dot_general(a, b, (((0,), (0,)), ((), ())), preferred_element_type=F32)


def _silu(x):
    return x / (1.0 + jnp.exp(-x))


def _rms(x, g):
    ms = jnp.mean(x * x, axis=-1, keepdims=True)
    return x * lax.rsqrt(ms + RMS_EPS) * g


def _ffn_body(x, g_ref, wg_ref, wu_ref, wd_ref, o_ref, act_ref):
    h = _rms(x, g_ref[...]).astype(BF16)
    for j in range(D_FF // FF_TILE):
        cols = slice(j * FF_TILE, (j + 1) * FF_TILE)
        gate = _dot(h, wg_ref[:, cols])
        up = _dot(h, wu_ref[:, cols])
        act_ref[:, cols] = (_silu(gate) * up).astype(BF16)
    y = _dot(act_ref[...], wd_ref[...])
    o_ref[...] = x + 0.5 * y


def _ffn_kernel(x_ref, g_ref, wg_ref, wu_ref, wd_ref, o_ref, act_ref):
    _ffn_body(x_ref[...], g_ref, wg_ref, wu_ref, wd_ref, o_ref, act_ref)


def _mix_ffn_kernel(x_ref, a_ref, r_ref, wo_ref, g_ref, wg_ref, wu_ref, wd_ref, o_ref, act_ref):
    x = (x_ref[...] + _dot(a_ref[...], wo_ref[:ATTN_WIDTH, :])
         + _dot(r_ref[...], wo_ref[ATTN_WIDTH:, :]))
    _ffn_body(x, g_ref, wg_ref, wu_ref, wd_ref, o_ref, act_ref)


def _resident(shape):
    return pl.BlockSpec(shape, lambda *_: (0,) * len(shape), pipeline_mode=pl.Buffered(1))


def _rows(width):
    return pl.BlockSpec((ROW_TILE, width), lambda i: (i, 0))


def _ffn_call(x, g, wg, wu, wd, mix=None):
    m = x.shape[0]
    w_specs = [_resident((1, D_MODEL)), _resident((D_MODEL, D_FF)),
               _resident((D_MODEL, D_FF)), _resident((D_FF, D_MODEL))]
    if mix is None:
        kern, ins, specs = _ffn_kernel, (x, g, wg, wu, wd), [_rows(D_MODEL)] + w_specs
    else:
        a, r, wo = mix
        kern = _mix_ffn_kernel
        ins = (x, a, r, wo, g, wg, wu, wd)
        specs = [_rows(D_MODEL), _rows(ATTN_WIDTH), _rows(HGRN_WIDTH),
                 _resident((D_MODEL, D_MODEL))] + w_specs
    return pl.pallas_call(
        kern,
        grid=(m // ROW_TILE,),
        in_specs=specs,
        out_specs=_rows(D_MODEL),
        out_shape=jax.ShapeDtypeStruct((m, D_MODEL), F32),
        scratch_shapes=[pltpu.VMEM((ROW_TILE, D_FF), BF16)],
        compiler_params=pltpu.CompilerParams(
            dimension_semantics=("arbitrary",), vmem_limit_bytes=VMEM_LIMIT),
        name="mix_ffn" if mix is not None else "ffn",
    )(*ins)


def _head_mean_square(a, seg_ref):
    sq = a * a
    hi = sq.astype(BF16)
    lo = (sq - hi.astype(F32)).astype(BF16)
    return (_dot(hi, seg_ref[...]) + _dot(lo, seg_ref[...])) * (1.0 / ATTN_HEAD_DIM)


def _proj_kernel(x_ref, g_ref, w_ref, qg_ref, kg_ref, lbp_ref, seg_ref,
                 q_ref, k_ref, v_ref, rq_ref, rk_ref, lf_ref, rv_ref, sg_ref):
    h = _rms(x_ref[...], g_ref[...]).astype(BF16)

    def cols(i):
        return _dot(h, w_ref[:, i * ATTN_WIDTH:(i + 1) * ATTN_WIDTH])

    aq = cols(0)
    scale = ATTN_HEAD_DIM ** -0.5
    q_ref[...] = (aq * lax.rsqrt(_head_mean_square(aq, seg_ref) + RMS_EPS)
                  * (qg_ref[...] * scale)).astype(BF16)
    ak = cols(1)
    k_ref[...] = (ak * lax.rsqrt(_head_mean_square(ak, seg_ref) + RMS_EPS)
                  * kg_ref[...]).astype(BF16)
    v_ref[...] = cols(2).astype(BF16)
    rq_ref[...] = _silu(cols(3)).astype(BF16)

    lbp = lbp_ref[...]
    e = jnp.exp(lbp - jnp.max(lbp, axis=0, keepdims=True))
    lb = e[0:1, :] / jnp.sum(e, axis=0, keepdims=True)
    hf = cols(4)
    en = jnp.exp(-jnp.abs(hf))
    big, small = 1.0 / (1.0 + en), en / (1.0 + en)
    pos = hf >= 0
    sig = jnp.where(pos, big, small)
    nsig = jnp.where(pos, small, big)
    f = lb + (1.0 - lb) * sig
    lf_ref[...] = jnp.log(f)
    rk_ref[...] = ((1.0 - lb) * nsig).astype(BF16)
    rv_ref[...] = cols(5).astype(BF16)
    sg_ref[...] = _silu(cols(6)).astype(BF16)


def _proj_call(x1, g, w_in, qg, kg, lbp, seg):
    m = x1.shape[0]
    half = lambda dt: jax.ShapeDtypeStruct((m, ATTN_WIDTH), dt)
    out_dtypes = [BF16, BF16, BF16, BF16, BF16, F32, BF16, BF16]
    return pl.pallas_call(
        _proj_kernel,
        grid=(m // ROW_TILE,),
        in_specs=[_rows(D_MODEL), _resident((1, D_MODEL)), _resident((D_MODEL, PROJ_COLS)),
                  _resident((1, ATTN_WIDTH)), _resident((1, ATTN_WIDTH)),
                  _resident(lbp.shape), _resident((ATTN_WIDTH, ATTN_WIDTH))],
        out_specs=[_rows(ATTN_WIDTH)] * 8,
        out_shape=[half(dt) for dt in out_dtypes],
        compiler_params=pltpu.CompilerParams(
            dimension_semantics=("arbitrary",), vmem_limit_bytes=VMEM_LIMIT),
        name="proj",
    )(x1, g, w_in, qg, kg, lbp, seg)


def _attn_kernel(q_ref, k_ref, v_ref, bias_ref, o_ref, kpad_ref, vpad_ref):
    cg = pl.program_id(1)

    @pl.when(cg == 0)
    def _():
        zeros = jnp.zeros((PAD, ATTN_WIDTH), BF16)
        kpad_ref[:PAD, :] = zeros
        vpad_ref[:PAD, :] = zeros
        kpad_ref[PAD:, :] = k_ref[0]
        vpad_ref[PAD:, :] = v_ref[0]

    lane = lax.broadcasted_iota(jnp.int32, (CHUNK, LANES), 1)
    even = lane < ATTN_HEAD_DIM
    key = lax.broadcasted_iota(jnp.int32, (2 * CHUNK, BAND), 1)
    for ci in range(ATTN_Q_CHUNKS):
        c = cg * ATTN_Q_CHUNKS + ci
        start = pl.multiple_of(c * CHUNK, CHUNK)
        valid = key >= PAD - c * CHUNK
        rows = slice(ci * CHUNK, (ci + 1) * CHUNK)
        for p in range(ATTN_HEADS // 2):
            cols = slice(p * LANES, (p + 1) * LANES)
            kw = kpad_ref[pl.ds(start, BAND), cols]
            vw = vpad_ref[pl.ds(start, BAND), cols]
            qp = q_ref[0, rows, cols]
            zero = jnp.zeros_like(qp)
            q2 = jnp.concatenate([jnp.where(even, qp, zero), jnp.where(even, zero, qp)], axis=0)
            s = _dot_nt(q2, kw) + bias_ref[p]
            s = jnp.where(valid, s, -jnp.inf)
            e = jnp.exp(s - jnp.max(s, axis=-1, keepdims=True))
            pv = _dot(e.astype(BF16), vw) / jnp.sum(e, axis=-1, keepdims=True)
            o_ref[0, rows, cols] = jnp.where(even, pv[:CHUNK], pv[CHUNK:]).astype(BF16)


def _attn_call(q, k, v, bias):
    b, s, _ = q.shape
    tq = ATTN_Q_CHUNKS * CHUNK
    return pl.pallas_call(
        _attn_kernel,
        grid=(b, s // tq),
        in_specs=[pl.BlockSpec((1, tq, ATTN_WIDTH), lambda i, j: (i, j, 0)),
                  pl.BlockSpec((1, s, ATTN_WIDTH), lambda i, j: (i, 0, 0)),
                  pl.BlockSpec((1, s, ATTN_WIDTH), lambda i, j: (i, 0, 0)),
                  _resident(bias.shape)],
        out_specs=pl.BlockSpec((1, tq, ATTN_WIDTH), lambda i, j: (i, j, 0)),
        out_shape=jax.ShapeDtypeStruct((b, s, ATTN_WIDTH), BF16),
        scratch_shapes=[pltpu.VMEM((s + PAD, ATTN_WIDTH), BF16),
                        pltpu.VMEM((s + PAD, ATTN_WIDTH), BF16)],
        compiler_params=pltpu.CompilerParams(
            dimension_semantics=("arbitrary", "arbitrary"), vmem_limit_bytes=VMEM_LIMIT),
        name="band_attn",
    )(q, k, v, bias)


def _level_sum_matrix():
    u = np.arange(CHUNK)[:, None]
    r = np.arange(CHUNK)[None, :]
    mats = [(r <= u)]
    for lv in range(N_LEVELS):
        c = CHUNK >> (lv + 1)
        m = (u // (2 * c)) * (2 * c) + c
        upper = (u & c) != 0
        mats.append(np.where(upper, (r > m) & (r <= u), (r > u) & (r <= m)))
    return np.concatenate(mats, axis=0).astype(np.float32)


def _hgrn_kernel(rq_ref, rk_ref, lf_ref, rv_ref, sg_ref, lsum_ref, og_ref, o_ref, state_ref):
    @pl.when(pl.program_id(1) == 0)
    def _():
        state_ref[...] = jnp.zeros_like(state_ref)

    t_row = lax.broadcasted_iota(jnp.int32, (CHUNK, HGRN_HEAD_DIM), 0)
    t_sq = lax.broadcasted_iota(jnp.int32, (CHUNK, CHUNK), 0)
    s_sq = lax.broadcasted_iota(jnp.int32, (CHUNK, CHUNK), 1)
    tx = t_sq ^ s_sq

    def chunk_step(i, carry):
        rows = pl.ds(pl.multiple_of(i * CHUNK, CHUNK), CHUNK)
        g = lf_ref[0, rows, :]
        g_hi = g.astype(BF16)
        g_lo = (g - g_hi.astype(F32)).astype(BF16)
        esum = _dot(lsum_ref[...], g_hi) + _dot(lsum_ref[...], g_lo)
        for hd in range(HGRN_HEADS):
            cols = slice(hd * HGRN_HEAD_DIM, (hd + 1) * HGRN_HEAD_DIM)
            q = rq_ref[0, rows, cols].astype(F32)
            k = rk_ref[0, rows, cols].astype(F32)
            v = rv_ref[0, rows, cols]
            b = esum[:CHUNK, cols]
            b_last = b[CHUNK - 1:CHUNK, :]
            state = state_ref[hd]
            o = _dot_nt((q * jnp.exp(b)).astype(BF16), state.astype(BF16))
            scores = jnp.where(t_sq == s_sq, jnp.sum(q * k, axis=-1, keepdims=True), 0.0)
            for lv in range(N_LEVELS):
                c = CHUNK >> (lv + 1)
                w = jnp.exp(esum[(lv + 1) * CHUNK:(lv + 2) * CHUNK, cols])
                upper = (t_row & c) != 0
                qw = jnp.where(upper, q * w, 0.0).astype(BF16)
                kw = jnp.where(upper, 0.0, k * w).astype(BF16)
                scores = scores + jnp.where(tx < 2 * c, _dot_nt(qw, kw), 0.0)
            o = o + _dot(scores.astype(BF16), v)
            k_up = (k * jnp.exp(b_last - b)).astype(BF16)
            state_ref[hd] = state * jnp.exp(b_last) + _dot_tn(v, k_up)
            y = _rms(o, og_ref[...]) * sg_ref[0, rows, cols].astype(F32)
            o_ref[0, rows, cols] = y.astype(BF16)
        return carry

    lax.fori_loop(0, HGRN_CHUNKS, chunk_step, 0)


def _hgrn_call(rq, rk, lf, rv, sg, lsum, og):
    b, s, _ = rq.shape
    rows = HGRN_CHUNKS * CHUNK
    blk = pl.BlockSpec((1, rows, HGRN_WIDTH), lambda i, j: (i, j, 0))
    return pl.pallas_call(
        _hgrn_kernel,
        grid=(b, s // rows),
        in_specs=[blk] * 5 + [_resident(lsum.shape), _resident((1, HGRN_HEAD_DIM))],
        out_specs=blk,
        out_shape=jax.ShapeDtypeStruct((b, s, HGRN_WIDTH), BF16),
        scratch_shapes=[pltpu.VMEM((HGRN_HEADS, HGRN_HEAD_DIM, HGRN_HEAD_DIM), F32)],
        compiler_params=pltpu.CompilerParams(
            dimension_semantics=("arbitrary", "arbitrary"), vmem_limit_bytes=VMEM_LIMIT),
        name="hgrn2",
    )(rq, rk, lf, rv, sg, lsum, og)


def _rel_bias_table(rel_bias):
    t_pos = jnp.arange(CHUNK)[:, None] + PAD
    s_pos = jnp.arange(BAND)[None, :]
    rel_idx = jnp.clip(t_pos - s_pos, -REL_CLIP, REL_CLIP) + REL_CLIP
    bias = rel_bias.astype(F32)[:, rel_idx]
    return bias.reshape(ATTN_HEADS // 2, 2 * CHUNK, BAND)


def kernel(x, ffn1_norm_g, ffn1_w_gate, ffn1_w_up, ffn1_w_down, mix_norm_g, w_in,
           attn_q_norm_g, attn_k_norm_g, attn_rel_bias, hgrn_lower_bounds, hgrn_out_norm_g,
           w_out, ffn2_norm_g, ffn2_w_gate, ffn2_w_up, ffn2_w_down):
    bsz, seq, _ = x.shape
    depth = ffn1_norm_g.shape[0]
    assert depth == 1 and seq % (HGRN_CHUNKS * CHUNK) == 0 and (bsz * seq) % ROW_TILE == 0
    head_of_col = np.arange(ATTN_WIDTH) // ATTN_HEAD_DIM
    seg = jnp.asarray(head_of_col[:, None] == head_of_col[None, :], dtype=BF16)
    lsum = jnp.asarray(_level_sum_matrix(), dtype=BF16)
    row = lambda g: g.reshape(1, -1).astype(F32)
    tile_heads = lambda g: jnp.tile(g.astype(F32), ATTN_HEADS).reshape(1, ATTN_WIDTH)

    xf = x.reshape(bsz * seq, D_MODEL)
    for l in range(depth):
        x1 = _ffn_call(xf, row(ffn1_norm_g[l]), ffn1_w_gate[l].astype(BF16),
                       ffn1_w_up[l].astype(BF16), ffn1_w_down[l].astype(BF16))
        q, k, v, rq, rk, lf, rv, sg = _proj_call(
            x1, row(mix_norm_g[l]), w_in[l].astype(BF16), tile_heads(attn_q_norm_g[l]),
            tile_heads(attn_k_norm_g[l]), hgrn_lower_bounds.astype(F32), seg)
        shp = (bsz, seq, ATTN_WIDTH)
        attn = _attn_call(q.reshape(shp), k.reshape(shp), v.reshape(shp),
                          _rel_bias_table(attn_rel_bias[l]))
        hg = _hgrn_call(rq.reshape(shp), rk.reshape(shp), lf.reshape(shp), rv.reshape(shp),
                        sg.reshape(shp), lsum, row(hgrn_out_norm_g[l]))
        xf = _ffn_call(x1, row(ffn2_norm_g[l]), ffn2_w_gate[l].astype(BF16),
                       ffn2_w_up[l].astype(BF16), ffn2_w_down[l].astype(BF16),
                       mix=(attn.reshape(-1, ATTN_WIDTH), hg.reshape(-1, HGRN_WIDTH),
                            w_out[l].astype(BF16)))
    return xf.reshape(bsz, seq, D_MODEL)
```

```python
import functools

import numpy as np
import jax
import jax.numpy as jnp
from jax import lax
from jax.experimental import pallas as pl
from jax.experimental.pallas import tpu as pltpu

D_MODEL = 1024
CHUNK = 64
ATTN_WIDTH = 512
HGRN_WIDTH = 512
ATTN_HEAD_DIM = 64
ATTN_HEADS = 8
HGRN_HEAD_DIM = 128
HGRN_HEADS = 4
LEFT_CHUNKS = 8
BAND = (LEFT_CHUNKS + 1) * CHUNK
PAD = LEFT_CHUNKS * CHUNK
REL_CLIP = 128
D_FF = 2816
RMS_EPS = 1e-6
PROJ_COLS = 3 * ATTN_WIDTH + 4 * HGRN_WIDTH

LANES = 128
MXU_TILE = 256
FF_TILE = MXU_TILE
ROW_TILE = 512
SUBLANES = 8
ATTN_Q_CHUNKS = 2
ATTN_TQ = ATTN_Q_CHUNKS * CHUNK
ATTN_WIN = PAD + ATTN_TQ
HGRN_CHUNKS = 8
HGRN_UNROLL = 2
N_LEVELS = 6
VMEM_LIMIT = 56 * 1024 * 1024

BF16 = jnp.bfloat16
F32 = jnp.float32


def _dot(a, b):
    return jnp.dot(a, b, preferred_element_type=F32)


def _dot_nt(a, b):
    return lax.dot_general(a, b, (((1,), (1,)), ((), ())), preferred_element_type=F32)


def _dot_tn(a, b):
    return lax.dot_general(a, b, (((0,), (0,)), ((), ())), preferred_element_type=F32)


def _silu(x):
    return x / (1.0 + jnp.exp(-x))


def _rms(x, g):
    ms = jnp.mean(x * x, axis=-1, keepdims=True)
    return x * lax.rsqrt(ms + RMS_EPS) * g


def _ffn_body(x, g_ref, wg_ref, wu_ref, wd_ref, o_ref, act_ref):
    h = _rms(x, g_ref[...]).astype(BF16)
    for j in range(D_FF // FF_TILE):
        cols = slice(j * FF_TILE, (j + 1) * FF_TILE)
        gate = _dot(h, wg_ref[:, cols])
        up = _dot(h, wu_ref[:, cols])
        act_ref[:, cols] = (_silu(gate) * up).astype(BF16)
    y = _dot(act_ref[...], wd_ref[...])
    o_ref[...] = x + 0.5 * y


def _ffn_kernel(x_ref, g_ref, wg_ref, wu_ref, wd_ref, o_ref, act_ref):
    _ffn_body(x_ref[...], g_ref, wg_ref, wu_ref, wd_ref, o_ref, act_ref)


def _mix_ffn_kernel(x_ref, a_ref, r_ref, wo_ref, g_ref, wg_ref, wu_ref, wd_ref, o_ref, act_ref):
    x = (x_ref[...] + _dot(a_ref[...], wo_ref[:ATTN_WIDTH, :])
         + _dot(r_ref[...], wo_ref[ATTN_WIDTH:, :]))
    _ffn_body(x, g_ref, wg_ref, wu_ref, wd_ref, o_ref, act_ref)


def _resident(shape):
    return pl.BlockSpec(shape, lambda *_: (0,) * len(shape), pipeline_mode=pl.Buffered(1))


def _rows(width):
    return pl.BlockSpec((ROW_TILE, width), lambda i: (i, 0))


def _ffn_call(x, g, wg, wu, wd, mix=None):
    m = x.shape[0]
    w_specs = [_resident((1, D_MODEL)), _resident((D_MODEL, D_FF)),
               _resident((D_MODEL, D_FF)), _resident((D_FF, D_MODEL))]
    if mix is None:
        kern, ins, specs = _ffn_kernel, (x, g, wg, wu, wd), [_rows(D_MODEL)] + w_specs
    else:
        a, r, wo = mix
        kern = _mix_ffn_kernel
        ins = (x, a, r, wo, g, wg, wu, wd)
        specs = [_rows(D_MODEL), _rows(ATTN_WIDTH), _rows(HGRN_WIDTH),
                 _resident((D_MODEL, D_MODEL))] + w_specs
    return pl.pallas_call(
        kern,
        grid=(m // ROW_TILE,),
        in_specs=specs,
        out_specs=_rows(D_MODEL),
        out_shape=jax.ShapeDtypeStruct((m, D_MODEL), F32),
        scratch_shapes=[pltpu.VMEM((ROW_TILE, D_FF), BF16)],
        compiler_params=pltpu.CompilerParams(
            dimension_semantics=("arbitrary",), vmem_limit_bytes=VMEM_LIMIT),
        name="mix_ffn" if mix is not None else "ffn",
    )(*ins)


def _head_mean_square(a, seg_ref):
    return _dot((a * a).astype(BF16), seg_ref[...]) * (1.0 / ATTN_HEAD_DIM)


def _proj_kernel(x_ref, g_ref, w_ref, qg_ref, kg_ref, lbp_ref, seg_ref,
                 q_ref, k_ref, v_ref, rq_ref, rk_ref, lf_ref, rv_ref, sg_ref):
    h = _rms(x_ref[...], g_ref[...]).astype(BF16)

    def cols(i):
        return _dot(h, w_ref[:, i * ATTN_WIDTH:(i + 1) * ATTN_WIDTH])

    aq = cols(0)
    scale = ATTN_HEAD_DIM ** -0.5
    q_ref[...] = (aq * lax.rsqrt(_head_mean_square(aq, seg_ref) + RMS_EPS)
                  * (qg_ref[...] * scale)).astype(BF16)
    ak = cols(1)
    k_ref[...] = (ak * lax.rsqrt(_head_mean_square(ak, seg_ref) + RMS_EPS)
                  * kg_ref[...]).astype(BF16)
    v_ref[...] = cols(2).astype(BF16)
    rq_ref[...] = _silu(cols(3)).astype(BF16)

    lbp = lbp_ref[...]
    e = jnp.exp(lbp - jnp.max(lbp, axis=0, keepdims=True))
    lb = e[0:1, :] / jnp.sum(e, axis=0, keepdims=True)
    hf = cols(4)
    en = jnp.exp(-jnp.abs(hf))
    big, small = 1.0 / (1.0 + en), en / (1.0 + en)
    pos = hf >= 0
    sig = jnp.where(pos, big, small)
    nsig = jnp.where(pos, small, big)
    f = lb + (1.0 - lb) * sig
    lf_ref[...] = jnp.log(f)
    rk_ref[...] = ((1.0 - lb) * nsig).astype(BF16)
    rv_ref[...] = cols(5).astype(BF16)
    sg_ref[...] = _silu(cols(6)).astype(BF16)


def _proj_call(x1, g, w_in, qg, kg, lbp, seg):
    m = x1.shape[0]
    half = lambda dt: jax.ShapeDtypeStruct((m, ATTN_WIDTH), dt)
    out_dtypes = [BF16, BF16, BF16, BF16, BF16, F32, BF16, BF16]
    return pl.pallas_call(
        _proj_kernel,
        grid=(m // ROW_TILE,),
        in_specs=[_rows(D_MODEL), _resident((1, D_MODEL)), _resident((D_MODEL, PROJ_COLS)),
                  _resident((1, ATTN_WIDTH)), _resident((1, ATTN_WIDTH)),
                  _resident(lbp.shape), _resident((ATTN_WIDTH, ATTN_WIDTH))],
        out_specs=[_rows(ATTN_WIDTH)] * 8,
        out_shape=[half(dt) for dt in out_dtypes],
        compiler_params=pltpu.CompilerParams(
            dimension_semantics=("arbitrary",), vmem_limit_bytes=VMEM_LIMIT),
        name="proj",
    )(x1, g, w_in, qg, kg, lbp, seg)


def _attn_kernel(q_ref, k_ref, v_ref, bias_ref, o_ref, kpad_ref, vt_ref):
    cg = pl.program_id(1)
    seq = k_ref.shape[1]

    @pl.when(cg == 0)
    def _():
        kpad_ref[:PAD, :] = jnp.zeros((PAD, ATTN_WIDTH), BF16)
        kpad_ref[PAD:, :] = k_ref[0]
        vt_ref[:, :PAD] = jnp.zeros((ATTN_WIDTH, PAD), BF16)
        for r in range(0, seq, MXU_TILE):
            vt_ref[:, PAD + r:PAD + r + MXU_TILE] = v_ref[0, r:r + MXU_TILE, :].T

    start = pl.multiple_of(cg * ATTN_TQ, ATTN_TQ)
    lane = lax.broadcasted_iota(jnp.int32, (ATTN_TQ, LANES), 1)
    even = lane < ATTN_HEAD_DIM
    key = lax.broadcasted_iota(jnp.int32, (ATTN_WIN, 2 * ATTN_TQ), 0)
    valid = key >= PAD - cg * ATTN_TQ
    for p in range(ATTN_HEADS // 2):
        cols = slice(p * LANES, (p + 1) * LANES)
        kw = kpad_ref[pl.ds(start, ATTN_WIN), cols]
        vtw = vt_ref[cols, pl.ds(start, ATTN_WIN)]
        qp = q_ref[0, :, cols]
        zero = jnp.zeros_like(qp)
        q2 = jnp.concatenate([jnp.where(even, qp, zero), jnp.where(even, zero, qp)], axis=0)
        st = _dot_nt(kw, q2) + bias_ref[p]
        st = jnp.where(valid, st, -jnp.inf)
        e = jnp.exp(st - jnp.max(st, axis=0, keepdims=True))
        ot = _dot(vtw, e.astype(BF16)) / jnp.sum(e, axis=0, keepdims=True)
        same_head = jnp.concatenate([ot[:ATTN_HEAD_DIM, :ATTN_TQ], ot[ATTN_HEAD_DIM:, ATTN_TQ:]], axis=0)
        o_ref[0, :, cols] = same_head.T.astype(BF16)


def _attn_call(q, k, v, bias):
    b, s, _ = q.shape
    return pl.pallas_call(
        _attn_kernel,
        grid=(b, s // ATTN_TQ),
        in_specs=[pl.BlockSpec((1, ATTN_TQ, ATTN_WIDTH), lambda i, j: (i, j, 0)),
                  pl.BlockSpec((1, s, ATTN_WIDTH), lambda i, j: (i, 0, 0)),
                  pl.BlockSpec((1, s, ATTN_WIDTH), lambda i, j: (i, 0, 0)),
                  _resident(bias.shape)],
        out_specs=pl.BlockSpec((1, ATTN_TQ, ATTN_WIDTH), lambda i, j: (i, j, 0)),
        out_shape=jax.ShapeDtypeStruct((b, s, ATTN_WIDTH), BF16),
        scratch_shapes=[pltpu.VMEM((s + PAD, ATTN_WIDTH), BF16),
                        pltpu.VMEM((ATTN_WIDTH, s + PAD), BF16)],
        compiler_params=pltpu.CompilerParams(
            dimension_semantics=("arbitrary", "arbitrary"), vmem_limit_bytes=VMEM_LIMIT),
        name="band_attn",
    )(q, k, v, bias)


def _level_sum_matrix():
    u = np.arange(CHUNK)[:, None]
    r = np.arange(CHUNK)[None, :]
    mats = [(r <= u)]
    for lv in range(N_LEVELS):
        c = CHUNK >> (lv + 1)
        m = (u // (2 * c)) * (2 * c) + c
        upper = (u & c) != 0
        mats.append(np.where(upper, (r > m) & (r <= u), (r > u) & (r <= m)))
    return np.concatenate(mats, axis=0).astype(np.float32)


def _hgrn_kernel(rq_ref, rk_ref, lf_ref, rv_ref, sg_ref, lsum_ref, og_ref, o_ref, state_ref):
    @pl.when(pl.program_id(1) == 0)
    def _():
        state_ref[...] = jnp.zeros_like(state_ref)

    t_row = lax.broadcasted_iota(jnp.int32, (CHUNK, HGRN_HEAD_DIM), 0)
    t_sq = lax.broadcasted_iota(jnp.int32, (CHUNK, CHUNK), 0)
    s_sq = lax.broadcasted_iota(jnp.int32, (CHUNK, CHUNK), 1)
    tx = t_sq ^ s_sq

    def level_operands(q, k, w, c):
        if c >= SUBLANES:
            qs, ks = [], []
            zero = jnp.zeros((c, HGRN_HEAD_DIM), F32)
            for blk in range(CHUNK // c):
                r = slice(blk * c, (blk + 1) * c)
                qs.append(q[r] * w[r] if blk % 2 else zero)
                ks.append(zero if blk % 2 else k[r] * w[r])
            return jnp.concatenate(qs, axis=0).astype(BF16), jnp.concatenate(ks, axis=0).astype(BF16)
        upper = (t_row & c) != 0
        return (jnp.where(upper, q * w, 0.0).astype(BF16), jnp.where(upper, 0.0, k * w).astype(BF16))

    def chunk_step(i, carry):
        rows = pl.ds(pl.multiple_of(i * CHUNK, CHUNK), CHUNK)
        esum = _dot(lsum_ref[...], lf_ref[0, rows, :].astype(BF16))
        for hd in range(HGRN_HEADS):
            cols = slice(hd * HGRN_HEAD_DIM, (hd + 1) * HGRN_HEAD_DIM)
            q = rq_ref[0, rows, cols].astype(F32)
            k = rk_ref[0, rows, cols].astype(F32)
            v = rv_ref[0, rows, cols]
            b = esum[:CHUNK, cols]
            b_last = b[CHUNK - 1:CHUNK, :]
            state = state_ref[hd]
            o = _dot_nt((q * jnp.exp(b)).astype(BF16), state.astype(BF16))
            scores = jnp.where(t_sq == s_sq, jnp.sum(q * k, axis=-1, keepdims=True), 0.0)
            for lv in range(N_LEVELS):
                c = CHUNK >> (lv + 1)
                w = jnp.exp(esum[(lv + 1) * CHUNK:(lv + 2) * CHUNK, cols])
                qw, kw = level_operands(q, k, w, c)
                scores = scores + jnp.where(tx < 2 * c, _dot_nt(qw, kw), 0.0)
            o = o + _dot(scores.astype(BF16), v)
            k_up = (k * jnp.exp(b_last - b)).astype(BF16)
            state_ref[hd] = state * jnp.exp(b_last) + _dot_tn(v, k_up)
            y = _rms(o, og_ref[...]) * sg_ref[0, rows, cols].astype(F32)
            o_ref[0, rows, cols] = y.astype(BF16)
        return carry

    lax.fori_loop(0, HGRN_CHUNKS, chunk_step, 0, unroll=HGRN_UNROLL)


def _hgrn_call(rq, rk, lf, rv, sg, lsum, og):
    b, s, _ = rq.shape
    rows = HGRN_CHUNKS * CHUNK
    blk = pl.BlockSpec((1, rows, HGRN_WIDTH), lambda i, j: (i, j, 0))
    return pl.pallas_call(
        _hgrn_kernel,
        grid=(b, s // rows),
        in_specs=[blk] * 5 + [_resident(lsum.shape), _resident((1, HGRN_HEAD_DIM))],
        out_specs=blk,
        out_shape=jax.ShapeDtypeStruct((b, s, HGRN_WIDTH), BF16),
        scratch_shapes=[pltpu.VMEM((HGRN_HEADS, HGRN_HEAD_DIM, HGRN_HEAD_DIM), F32)],
        compiler_params=pltpu.CompilerParams(
            dimension_semantics=("arbitrary", "arbitrary"), vmem_limit_bytes=VMEM_LIMIT),
        name="hgrn2",
    )(rq, rk, lf, rv, sg, lsum, og)


def _rel_bias_table(rel_bias):
    assert ATTN_TQ - 1 <= REL_CLIP
    rb = rel_bias.astype(F32)
    n = ATTN_WIN + ATTN_TQ
    far = jnp.broadcast_to(rb[:, 2 * REL_CLIP:], (ATTN_HEADS, ATTN_WIN - REL_CLIP))
    near = rb[:, REL_CLIP - ATTN_TQ + 1:2 * REL_CLIP][:, ::-1]
    x = jnp.concatenate([far, near, jnp.zeros((ATTN_HEADS, 1), F32)], axis=1)
    rolled = jnp.tile(x, (1, ATTN_TQ))[:, :ATTN_TQ * (n - 1)].reshape(ATTN_HEADS, ATTN_TQ, n - 1)
    tbl = rolled[:, :, ATTN_TQ - 1:ATTN_TQ - 1 + ATTN_WIN]
    qc = np.arange(ATTN_TQ)[:, None] // CHUNK
    kc = np.arange(ATTN_WIN)[None, :] // CHUNK
    tbl = jnp.where((kc >= qc) & (kc <= qc + LEFT_CHUNKS), tbl, -jnp.inf)
    tbl = tbl.reshape(ATTN_HEADS // 2, 2, ATTN_TQ, ATTN_WIN).transpose(0, 3, 1, 2)
    return tbl.reshape(ATTN_HEADS // 2, ATTN_WIN, 2 * ATTN_TQ)


def kernel(x, ffn1_norm_g, ffn1_w_gate, ffn1_w_up, ffn1_w_down, mix_norm_g, w_in,
           attn_q_norm_g, attn_k_norm_g, attn_rel_bias, hgrn_lower_bounds, hgrn_out_norm_g,
           w_out, ffn2_norm_g, ffn2_w_gate, ffn2_w_up, ffn2_w_down):
    bsz, seq, _ = x.shape
    depth = ffn1_norm_g.shape[0]
    assert depth == 1 and seq % (HGRN_CHUNKS * CHUNK) == 0 and (bsz * seq) % ROW_TILE == 0
    head_of_col = np.arange(ATTN_WIDTH) // ATTN_HEAD_DIM
    seg = jnp.asarray(head_of_col[:, None] == head_of_col[None, :], dtype=BF16)
    lsum = jnp.asarray(_level_sum_matrix(), dtype=BF16)
    row = lambda g: g.reshape(1, -1).astype(F32)
    tile_heads = lambda g: jnp.tile(g.astype(F32), ATTN_HEADS).reshape(1, ATTN_WIDTH)

    xf = x.reshape(bsz * seq, D_MODEL)
    for l in range(depth):
        x1 = _ffn_call(xf, row(ffn1_norm_g[l]), ffn1_w_gate[l].astype(BF16),
                       ffn1_w_up[l].astype(BF16), ffn1_w_down[l].astype(BF16))
        q, k, v, rq, rk, lf, rv, sg = _proj_call(
            x1, row(mix_norm_g[l]), w_in[l].astype(BF16), tile_heads(attn_q_norm_g[l]),
            tile_heads(attn_k_norm_g[l]), hgrn_lower_bounds.astype(F32), seg)
        shp = (bsz, seq, ATTN_WIDTH)
        attn = _attn_call(q.reshape(shp), k.reshape(shp), v.reshape(shp),
                          _rel_bias_table(attn_rel_bias[l]))
        hg = _hgrn_call(rq.reshape(shp), rk.reshape(shp), lf.reshape(shp), rv.reshape(shp),
                        sg.reshape(shp), lsum, row(hgrn_out_norm_g[l]))
        xf = _ffn_call(x1, row(ffn2_norm_g[l]), ffn2_w_gate[l].astype(BF16),
                       ffn2_w_up[l].astype(BF16), ffn2_w_down[l].astype(BF16),
                       mix=(attn.reshape(-1, ATTN_WIDTH), hg.reshape(-1, HGRN_WIDTH),
                            w_out[l].astype(BF16)))
    return xf.reshape(bsz, seq, D_MODEL)
```

```python
import functools

import numpy as np
import jax
import jax.numpy as jnp
from jax import lax
from jax.experimental import pallas as pl
from jax.experimental.pallas import tpu as pltpu

D_MODEL = 1024
CHUNK = 64
ATTN_WIDTH = 512
HGRN_WIDTH = 512
ATTN_HEAD_DIM = 64
ATTN_HEADS = 8
HGRN_HEAD_DIM = 128
HGRN_HEADS = 4
LEFT_CHUNKS = 8
BAND = (LEFT_CHUNKS + 1) * CHUNK
PAD = LEFT_CHUNKS * CHUNK
REL_CLIP = 128
D_FF = 2816
RMS_EPS = 1e-6
PROJ_COLS = 3 * ATTN_WIDTH + 4 * HGRN_WIDTH

LANES = 128
MXU_TILE = 256
FF_TILE = MXU_TILE
ROW_TILE = 512
SUBLANES = 8
BF16_ROWS = 2 * SUBLANES
LOG2E = float(np.log2(np.e))
ATTN_Q_CHUNKS = 2
ATTN_TQ = ATTN_Q_CHUNKS * CHUNK
ATTN_WIN = PAD + ATTN_TQ
HGRN_CHUNKS = 8
HGRN_UNROLL = 4
N_LEVELS = 6
VMEM_LIMIT = 56 * 1024 * 1024

BF16 = jnp.bfloat16
F32 = jnp.float32


def _dot(a, b):
    return jnp.dot(a, b, preferred_element_type=F32)


def _dot_nt(a, b):
    return lax.dot_general(a, b, (((1,), (1,)), ((), ())), preferred_element_type=F32)


def _dot_tn(a, b):
    return lax.dot_general(a, b, (((0,), (0,)), ((), ())), preferred_element_type=F32)


def _silu(x):
    return x / (1.0 + jnp.exp(-x))


def _rms(x, g):
    ms = jnp.mean(x * x, axis=-1, keepdims=True)
    return x * lax.rsqrt(ms + RMS_EPS) * g


def _ffn_body(x, g_ref, wg_ref, wu_ref, wd_ref, o_ref, act_ref):
    h = _rms(x, g_ref[...]).astype(BF16)
    for j in range(D_FF // FF_TILE):
        cols = slice(j * FF_TILE, (j + 1) * FF_TILE)
        gate = _dot(h, wg_ref[:, cols])
        up = _dot(h, wu_ref[:, cols])
        act_ref[:, cols] = (_silu(gate) * up).astype(BF16)
    y = _dot(act_ref[...], wd_ref[...])
    o_ref[...] = x + 0.5 * y


def _ffn_kernel(x_ref, g_ref, wg_ref, wu_ref, wd_ref, o_ref, act_ref):
    _ffn_body(x_ref[...], g_ref, wg_ref, wu_ref, wd_ref, o_ref, act_ref)


def _mix_ffn_kernel(x_ref, a_ref, r_ref, wo_ref, g_ref, wg_ref, wu_ref, wd_ref, o_ref, act_ref):
    x = (x_ref[...] + _dot(a_ref[...], wo_ref[:ATTN_WIDTH, :])
         + _dot(r_ref[...], wo_ref[ATTN_WIDTH:, :]))
    _ffn_body(x, g_ref, wg_ref, wu_ref, wd_ref, o_ref, act_ref)


def _resident(shape):
    return pl.BlockSpec(shape, lambda *_: (0,) * len(shape), pipeline_mode=pl.Buffered(1))


def _rows(width):
    return pl.BlockSpec((ROW_TILE, width), lambda i: (i, 0))


def _ffn_call(x, g, wg, wu, wd, mix=None):
    m = x.shape[0]
    w_specs = [_resident((1, D_MODEL)), _resident((D_MODEL, D_FF)),
               _resident((D_MODEL, D_FF)), _resident((D_FF, D_MODEL))]
    if mix is None:
        kern, ins, specs = _ffn_kernel, (x, g, wg, wu, wd), [_rows(D_MODEL)] + w_specs
    else:
        a, r, wo = mix
        kern = _mix_ffn_kernel
        ins = (x, a, r, wo, g, wg, wu, wd)
        specs = [_rows(D_MODEL), _rows(ATTN_WIDTH), _rows(HGRN_WIDTH),
                 _resident((D_MODEL, D_MODEL))] + w_specs
    return pl.pallas_call(
        kern,
        grid=(m // ROW_TILE,),
        in_specs=specs,
        out_specs=_rows(D_MODEL),
        out_shape=jax.ShapeDtypeStruct((m, D_MODEL), F32),
        scratch_shapes=[pltpu.VMEM((ROW_TILE, D_FF), BF16)],
        compiler_params=pltpu.CompilerParams(
            dimension_semantics=("arbitrary",), vmem_limit_bytes=VMEM_LIMIT),
        name="mix_ffn" if mix is not None else "ffn",
    )(*ins)


def _head_mean_square(a, seg_ref):
    return _dot((a * a).astype(BF16), seg_ref[...]) * (1.0 / ATTN_HEAD_DIM)


def _proj_kernel(x_ref, g_ref, w_ref, qg_ref, kg_ref, lbp_ref, seg_ref,
                 q_ref, k_ref, v_ref, rq_ref, rk_ref, lf_ref, rv_ref, sg_ref):
    h = _rms(x_ref[...], g_ref[...]).astype(BF16)

    def cols(i):
        return _dot(h, w_ref[:, i * ATTN_WIDTH:(i + 1) * ATTN_WIDTH])

    aq = cols(0)
    scale = ATTN_HEAD_DIM ** -0.5 * LOG2E
    q_ref[...] = (aq * lax.rsqrt(_head_mean_square(aq, seg_ref) + RMS_EPS)
                  * (qg_ref[...] * scale)).astype(BF16)
    ak = cols(1)
    k_ref[...] = (ak * lax.rsqrt(_head_mean_square(ak, seg_ref) + RMS_EPS)
                  * kg_ref[...]).astype(BF16)
    v_ref[...] = cols(2).astype(BF16)
    rq_ref[...] = _silu(cols(3)).astype(BF16)

    lbp = lbp_ref[...]
    e = jnp.exp(lbp - jnp.max(lbp, axis=0, keepdims=True))
    lb = e[0:1, :] / jnp.sum(e, axis=0, keepdims=True)
    hf = cols(4)
    en = jnp.exp(-jnp.abs(hf))
    big, small = 1.0 / (1.0 + en), en / (1.0 + en)
    pos = hf >= 0
    sig = jnp.where(pos, big, small)
    nsig = jnp.where(pos, small, big)
    f = lb + (1.0 - lb) * sig
    lf_ref[...] = jnp.log2(f)
    rk_ref[...] = ((1.0 - lb) * nsig).astype(BF16)
    rv_ref[...] = cols(5).astype(BF16)
    sg_ref[...] = _silu(cols(6)).astype(BF16)


def _proj_call(x1, g, w_in, qg, kg, lbp, seg):
    m = x1.shape[0]
    half = lambda dt: jax.ShapeDtypeStruct((m, ATTN_WIDTH), dt)
    out_dtypes = [BF16, BF16, BF16, BF16, BF16, F32, BF16, BF16]
    return pl.pallas_call(
        _proj_kernel,
        grid=(m // ROW_TILE,),
        in_specs=[_rows(D_MODEL), _resident((1, D_MODEL)), _resident((D_MODEL, PROJ_COLS)),
                  _resident((1, ATTN_WIDTH)), _resident((1, ATTN_WIDTH)),
                  _resident(lbp.shape), _resident((ATTN_WIDTH, ATTN_WIDTH))],
        out_specs=[_rows(ATTN_WIDTH)] * 8,
        out_shape=[half(dt) for dt in out_dtypes],
        compiler_params=pltpu.CompilerParams(
            dimension_semantics=("arbitrary",), vmem_limit_bytes=VMEM_LIMIT),
        name="proj",
    )(x1, g, w_in, qg, kg, lbp, seg)


def _attn_kernel(q_ref, k_ref, v_ref, bias_ref, o_ref, kpad_ref, vt_ref, ones_ref):
    cg = pl.program_id(1)
    seq = k_ref.shape[1]

    @pl.when(cg == 0)
    def _():
        kpad_ref[:PAD, :] = jnp.zeros((PAD, ATTN_WIDTH), BF16)
        kpad_ref[PAD:, :] = k_ref[0]
        vt_ref[:, :PAD] = jnp.zeros((ATTN_WIDTH, PAD), BF16)
        for r in range(0, seq, MXU_TILE):
            vt_ref[:, PAD + r:PAD + r + MXU_TILE] = v_ref[0, r:r + MXU_TILE, :].T
        ones_ref[:, :PAD] = jnp.zeros((BF16_ROWS, PAD), BF16)
        ones_ref[:, PAD:] = jnp.ones((BF16_ROWS, seq), BF16)

    start = pl.multiple_of(cg * ATTN_TQ, ATTN_TQ)
    window = pl.ds(start, ATTN_WIN)
    lane = lax.broadcasted_iota(jnp.int32, (ATTN_TQ, LANES), 1)
    even = lane < ATTN_HEAD_DIM
    first_valid = PAD // ATTN_TQ - cg

    def scores(p):
        cols = slice(p * LANES, (p + 1) * LANES)
        qp = q_ref[0, :, cols]
        zero = jnp.zeros_like(qp)
        q2 = jnp.concatenate([jnp.where(even, qp, zero), jnp.where(even, zero, qp)], axis=0)
        st = _dot_nt(kpad_ref[window, cols], q2) + bias_ref[p]
        return jnp.concatenate(
            [jnp.where(j >= first_valid, st[j * ATTN_TQ:(j + 1) * ATTN_TQ], -jnp.inf)
             for j in range(ATTN_WIN // ATTN_TQ)], axis=0)

    def weights(st):
        return jnp.exp2(st - jnp.max(st, axis=0, keepdims=True)).astype(BF16)

    def output(p, e):
        cols = slice(p * LANES, (p + 1) * LANES)
        vt1 = jnp.concatenate([vt_ref[cols, window], ones_ref[:, window]], axis=0)
        ot = _dot(vt1, e)
        ot = ot[:LANES] / ot[LANES:LANES + 1]
        same_head = jnp.concatenate([ot[:ATTN_HEAD_DIM, :ATTN_TQ], ot[ATTN_HEAD_DIM:, ATTN_TQ:]], axis=0)
        o_ref[0, :, cols] = same_head.T.astype(BF16)

    n_pairs = ATTN_HEADS // 2
    st_next = scores(0)
    for p in range(n_pairs):
        st = st_next
        if p + 1 < n_pairs:
            st_next = scores(p + 1)
        output(p, weights(st))


def _attn_call(q, k, v, bias):
    b, s, _ = q.shape
    return pl.pallas_call(
        _attn_kernel,
        grid=(b, s // ATTN_TQ),
        in_specs=[pl.BlockSpec((1, ATTN_TQ, ATTN_WIDTH), lambda i, j: (i, j, 0)),
                  pl.BlockSpec((1, s, ATTN_WIDTH), lambda i, j: (i, 0, 0)),
                  pl.BlockSpec((1, s, ATTN_WIDTH), lambda i, j: (i, 0, 0)),
                  _resident(bias.shape)],
        out_specs=pl.BlockSpec((1, ATTN_TQ, ATTN_WIDTH), lambda i, j: (i, j, 0)),
        out_shape=jax.ShapeDtypeStruct((b, s, ATTN_WIDTH), BF16),
        scratch_shapes=[pltpu.VMEM((s + PAD, ATTN_WIDTH), BF16),
                        pltpu.VMEM((ATTN_WIDTH, s + PAD), BF16),
                        pltpu.VMEM((BF16_ROWS, s + PAD), BF16)],
        compiler_params=pltpu.CompilerParams(
            dimension_semantics=("arbitrary", "arbitrary"), vmem_limit_bytes=VMEM_LIMIT),
        name="band_attn",
    )(q, k, v, bias)


def _level_sum_matrix():
    u = np.arange(CHUNK)[:, None]
    r = np.arange(CHUNK)[None, :]
    mats = [(r <= u)]
    for lv in range(N_LEVELS):
        c = CHUNK >> (lv + 1)
        m = (u // (2 * c)) * (2 * c) + c
        upper = (u & c) != 0
        mats.append(np.where(upper, (r > m) & (r <= u), (r > u) & (r <= m)))
    return np.concatenate(mats, axis=0).astype(np.float32)


def _hgrn_kernel(rq_ref, rk_ref, lf_ref, rv_ref, sg_ref, lsum_ref, og_ref, o_ref, state_ref):
    @pl.when(pl.program_id(1) == 0)
    def _():
        state_ref[...] = jnp.zeros_like(state_ref)

    t_row = lax.broadcasted_iota(jnp.int32, (CHUNK, HGRN_HEAD_DIM), 0)
    t_sq = lax.broadcasted_iota(jnp.int32, (CHUNK, CHUNK), 0)
    s_sq = lax.broadcasted_iota(jnp.int32, (CHUNK, CHUNK), 1)
    tx = t_sq ^ s_sq

    def level_operands(q, k, w, c):
        if c >= SUBLANES:
            qs, ks = [], []
            zero = jnp.zeros((c, HGRN_HEAD_DIM), F32)
            for blk in range(CHUNK // c):
                r = slice(blk * c, (blk + 1) * c)
                qs.append(q[r] * w[r] if blk % 2 else zero)
                ks.append(zero if blk % 2 else k[r] * w[r])
            return jnp.concatenate(qs, axis=0).astype(BF16), jnp.concatenate(ks, axis=0).astype(BF16)
        upper = (t_row & c) != 0
        return (jnp.where(upper, q * w, 0.0).astype(BF16), jnp.where(upper, 0.0, k * w).astype(BF16))

    def head_cols(hd):
        return slice(hd * HGRN_HEAD_DIM, (hd + 1) * HGRN_HEAD_DIM)

    def partial_sums(rows):
        return _dot(lsum_ref[...], lf_ref[0, rows, :].astype(BF16))

    def level_products(rows, hd, esum):
        cols = head_cols(hd)
        q = rq_ref[0, rows, cols].astype(F32)
        k = rk_ref[0, rows, cols].astype(F32)
        b = esum[:CHUNK, cols]
        b_last = b[CHUNK - 1:CHUNK, :]
        prods = []
        for lv in range(N_LEVELS):
            c = CHUNK >> (lv + 1)
            w = jnp.exp2(esum[(lv + 1) * CHUNK:(lv + 2) * CHUNK, cols])
            qw, kw = level_operands(q, k, w, c)
            prods.append(_dot_nt(qw, kw))
        return dict(q_in=(q * jnp.exp2(b)).astype(BF16), k_up=(k * jnp.exp2(b_last - b)).astype(BF16),
                    decay=jnp.exp2(b_last), diag=jnp.sum(q * k, axis=-1, keepdims=True), prods=prods)

    def finish(rows, hd, a):
        cols = head_cols(hd)
        scores = jnp.where(t_sq == s_sq, a["diag"], 0.0)
        for lv in range(N_LEVELS):
            scores = scores + jnp.where(tx < 2 * (CHUNK >> (lv + 1)), a["prods"][lv], 0.0)
        v = rv_ref[0, rows, cols]
        state = state_ref[hd]
        o = _dot_nt(a["q_in"], state.astype(BF16)) + _dot(scores.astype(BF16), v)
        state_ref[hd] = state * a["decay"] + _dot_tn(v, a["k_up"])
        y = _rms(o, og_ref[...]) * sg_ref[0, rows, cols].astype(F32)
        o_ref[0, rows, cols] = y.astype(BF16)

    def group_step(i, carry):
        rows = [pl.ds(pl.multiple_of((i * HGRN_UNROLL + u) * CHUNK, CHUNK), CHUNK)
                for u in range(HGRN_UNROLL)]
        units = [(u, hd) for u in range(HGRN_UNROLL) for hd in range(HGRN_HEADS)]
        esums = {0: partial_sums(rows[0])}
        nxt = level_products(rows[0], 0, esums[0])
        for n, (u, hd) in enumerate(units):
            cur = nxt
            if hd == 0 and u + 1 < HGRN_UNROLL:
                esums[u + 1] = partial_sums(rows[u + 1])
            if n + 1 < len(units):
                u2, hd2 = units[n + 1]
                nxt = level_products(rows[u2], hd2, esums[u2])
            finish(rows[u], hd, cur)
        return carry

    lax.fori_loop(0, HGRN_CHUNKS // HGRN_UNROLL, group_step, 0)


def _hgrn_call(rq, rk, lf, rv, sg, lsum, og):
    b, s, _ = rq.shape
    rows = HGRN_CHUNKS * CHUNK
    blk = pl.BlockSpec((1, rows, HGRN_WIDTH), lambda i, j: (i, j, 0))
    return pl.pallas_call(
        _hgrn_kernel,
        grid=(b, s // rows),
        in_specs=[blk] * 5 + [_resident(lsum.shape), _resident((1, HGRN_HEAD_DIM))],
        out_specs=blk,
        out_shape=jax.ShapeDtypeStruct((b, s, HGRN_WIDTH), BF16),
        scratch_shapes=[pltpu.VMEM((HGRN_HEADS, HGRN_HEAD_DIM, HGRN_HEAD_DIM), F32)],
        compiler_params=pltpu.CompilerParams(
            dimension_semantics=("arbitrary", "arbitrary"), vmem_limit_bytes=VMEM_LIMIT),
        name="hgrn2",
    )(rq, rk, lf, rv, sg, lsum, og)


def _rel_bias_table(rel_bias):
    assert ATTN_TQ - 1 <= REL_CLIP
    rb = rel_bias.astype(F32) * LOG2E
    n = ATTN_WIN + ATTN_TQ
    far = jnp.broadcast_to(rb[:, 2 * REL_CLIP:], (ATTN_HEADS, ATTN_WIN - REL_CLIP))
    near = rb[:, REL_CLIP - ATTN_TQ + 1:2 * REL_CLIP][:, ::-1]
    x = jnp.concatenate([far, near, jnp.zeros((ATTN_HEADS, 1), F32)], axis=1)
    rolled = jnp.tile(x, (1, ATTN_TQ))[:, :ATTN_TQ * (n - 1)].reshape(ATTN_HEADS, ATTN_TQ, n - 1)
    tbl = rolled[:, :, ATTN_TQ - 1:ATTN_TQ - 1 + ATTN_WIN]
    qc = np.arange(ATTN_TQ)[:, None] // CHUNK
    kc = np.arange(ATTN_WIN)[None, :] // CHUNK
    tbl = jnp.where((kc >= qc) & (kc <= qc + LEFT_CHUNKS), tbl, -jnp.inf)
    tbl = tbl.reshape(ATTN_HEADS // 2, 2, ATTN_TQ, ATTN_WIN).transpose(0, 3, 1, 2)
    return tbl.reshape(ATTN_HEADS // 2, ATTN_WIN, 2 * ATTN_TQ)


def kernel(x, ffn1_norm_g, ffn1_w_gate, ffn1_w_up, ffn1_w_down, mix_norm_g, w_in,
           attn_q_norm_g, attn_k_norm_g, attn_rel_bias, hgrn_lower_bounds, hgrn_out_norm_g,
           w_out, ffn2_norm_g, ffn2_w_gate, ffn2_w_up, ffn2_w_down):
    bsz, seq, _ = x.shape
    depth = ffn1_norm_g.shape[0]
    assert depth == 1 and seq % (HGRN_CHUNKS * CHUNK) == 0 and (bsz * seq) % ROW_TILE == 0
    head_of_col = np.arange(ATTN_WIDTH) // ATTN_HEAD_DIM
    seg = jnp.asarray(head_of_col[:, None] == head_of_col[None, :], dtype=BF16)
    lsum = jnp.asarray(_level_sum_matrix(), dtype=BF16)
    row = lambda g: g.reshape(1, -1).astype(F32)
    tile_heads = lambda g: jnp.tile(g.astype(F32), ATTN_HEADS).reshape(1, ATTN_WIDTH)

    xf = x.reshape(bsz * seq, D_MODEL)
    for l in range(depth):
        x1 = _ffn_call(xf, row(ffn1_norm_g[l]), ffn1_w_gate[l].astype(BF16),
                       ffn1_w_up[l].astype(BF16), ffn1_w_down[l].astype(BF16))
        q, k, v, rq, rk, lf, rv, sg = _proj_call(
            x1, row(mix_norm_g[l]), w_in[l].astype(BF16), tile_heads(attn_q_norm_g[l]),
            tile_heads(attn_k_norm_g[l]), hgrn_lower_bounds.astype(F32), seg)
        shp = (bsz, seq, ATTN_WIDTH)
        attn = _attn_call(q.reshape(shp), k.reshape(shp), v.reshape(shp),
                          _rel_bias_table(attn_rel_bias[l]))
        hg = _hgrn_call(rq.reshape(shp), rk.reshape(shp), lf.reshape(shp), rv.reshape(shp),
                        sg.reshape(shp), lsum, row(hgrn_out_norm_g[l]))
        xf = _ffn_call(x1, row(ffn2_norm_g[l]), ffn2_w_gate[l].astype(BF16),
                       ffn2_w_up[l].astype(BF16), ffn2_w_down[l].astype(BF16),
                       mix=(attn.reshape(-1, ATTN_WIDTH), hg.reshape(-1, HGRN_WIDTH),
                            w_out[l].astype(BF16)))
    return xf.reshape(bsz, seq, D_MODEL)
```

```python
import numpy as np
import jax
import jax.numpy as jnp
from jax import lax
from jax.experimental import pallas as pl
from jax.experimental.pallas import tpu as pltpu

D_MODEL = 1024
CHUNK = 64
ATTN_WIDTH = 512
HGRN_WIDTH = 512
ATTN_HEAD_DIM = 64
ATTN_HEADS = 8
HGRN_HEAD_DIM = 128
HGRN_HEADS = 4
LEFT_CHUNKS = 8
PAD = LEFT_CHUNKS * CHUNK
REL_CLIP = 128
D_FF = 2816
RMS_EPS = 1e-6
PROJ_COLS = 3 * ATTN_WIDTH + 4 * HGRN_WIDTH

LANES = 128
SUBLANES = 8
BF16_ROWS = 2 * SUBLANES
MXU_TILE = 256
FF_TILE = MXU_TILE
ROW_TILE = 512
LOG2E = float(np.log2(np.e))
ATTN_Q_CHUNKS = 2
ATTN_TQ = ATTN_Q_CHUNKS * CHUNK
ATTN_WIN = PAD + ATTN_TQ
MIX_ROWS = 256
N_LEVELS = 6
VMEM_LIMIT = 56 * 1024 * 1024

BF16 = jnp.bfloat16
F32 = jnp.float32


def _dot(a, b):
    return jnp.dot(a, b, preferred_element_type=F32)


def _dot_nt(a, b):
    return lax.dot_general(a, b, (((1,), (1,)), ((), ())), preferred_element_type=F32)


def _dot_tn(a, b):
    return lax.dot_general(a, b, (((0,), (0,)), ((), ())), preferred_element_type=F32)


def _silu(x):
    return x / (1.0 + jnp.exp(-x))


def _rms(x, g):
    ms = jnp.mean(x * x, axis=-1, keepdims=True)
    return x * lax.rsqrt(ms + RMS_EPS) * g


def _ffn_body(x, g_ref, wg_ref, wu_ref, wd_ref, o_ref, act_ref):
    h = _rms(x, g_ref[...]).astype(BF16)
    for j in range(D_FF // FF_TILE):
        cols = slice(j * FF_TILE, (j + 1) * FF_TILE)
        gate = _dot(h, wg_ref[:, cols])
        up = _dot(h, wu_ref[:, cols])
        act_ref[:, cols] = (_silu(gate) * up).astype(BF16)
    y = _dot(act_ref[...], wd_ref[...])
    o_ref[...] = x + 0.5 * y


def _ffn_kernel(x_ref, g_ref, wg_ref, wu_ref, wd_ref, o_ref, act_ref):
    _ffn_body(x_ref[...], g_ref, wg_ref, wu_ref, wd_ref, o_ref, act_ref)


def _mix_ffn_kernel(x_ref, m_ref, wo_ref, g_ref, wg_ref, wu_ref, wd_ref, o_ref, act_ref):
    x = x_ref[...] + _dot(m_ref[...], wo_ref[...])
    _ffn_body(x, g_ref, wg_ref, wu_ref, wd_ref, o_ref, act_ref)


def _resident(shape):
    return pl.BlockSpec(shape, lambda *_: (0,) * len(shape), pipeline_mode=pl.Buffered(1))


def _rows(width):
    return pl.BlockSpec((ROW_TILE, width), lambda i: (i, 0))


def _ffn_call(x, g, wg, wu, wd, mix=None):
    m = x.shape[0]
    w_specs = [_resident((1, D_MODEL)), _resident((D_MODEL, D_FF)),
               _resident((D_MODEL, D_FF)), _resident((D_FF, D_MODEL))]
    if mix is None:
        kern, ins, specs = _ffn_kernel, (x, g, wg, wu, wd), [_rows(D_MODEL)] + w_specs
    else:
        mixed, wo = mix
        kern = _mix_ffn_kernel
        ins = (x, mixed, wo, g, wg, wu, wd)
        specs = [_rows(D_MODEL), _rows(D_MODEL), _resident((D_MODEL, D_MODEL))] + w_specs
    return pl.pallas_call(
        kern,
        grid=(m // ROW_TILE,),
        in_specs=specs,
        out_specs=_rows(D_MODEL),
        out_shape=jax.ShapeDtypeStruct((m, D_MODEL), F32),
        scratch_shapes=[pltpu.VMEM((ROW_TILE, D_FF), BF16)],
        compiler_params=pltpu.CompilerParams(
            dimension_semantics=("arbitrary",), vmem_limit_bytes=VMEM_LIMIT),
        name="mix_ffn" if mix is not None else "ffn",
    )(*ins)


def _head_mean_square(a, seg_ref):
    sq = (a * a).astype(BF16)
    sums = [_dot(sq[:, j:j + MXU_TILE], seg_ref[...]) for j in range(0, ATTN_WIDTH, MXU_TILE)]
    return jnp.concatenate(sums, axis=1) * (1.0 / ATTN_HEAD_DIM)


def _proj_kernel(x_ref, g_ref, w_ref, qg_ref, kg_ref, lbp_ref, seg_ref,
                 q_ref, k_ref, v_ref, rq_ref, rk_ref, lf_ref, rv_ref, sg_ref):
    h = _rms(x_ref[...], g_ref[...]).astype(BF16)

    def cols(i):
        return _dot(h, w_ref[:, i * ATTN_WIDTH:(i + 1) * ATTN_WIDTH])

    aq = cols(0)
    scale = ATTN_HEAD_DIM ** -0.5 * LOG2E
    q_ref[...] = (aq * lax.rsqrt(_head_mean_square(aq, seg_ref) + RMS_EPS)
                  * (qg_ref[...] * scale)).astype(BF16)
    ak = cols(1)
    k_ref[...] = (ak * lax.rsqrt(_head_mean_square(ak, seg_ref) + RMS_EPS)
                  * kg_ref[...]).astype(BF16)
    v_ref[...] = cols(2).astype(BF16)
    rq_ref[...] = _silu(cols(3)).astype(BF16)

    lbp = lbp_ref[...]
    e = jnp.exp(lbp - jnp.max(lbp, axis=0, keepdims=True))
    lb = e[0:1, :] / jnp.sum(e, axis=0, keepdims=True)
    hf = cols(4)
    en = jnp.exp(-jnp.abs(hf))
    big, small = 1.0 / (1.0 + en), en / (1.0 + en)
    pos = hf >= 0
    sig = jnp.where(pos, big, small)
    nsig = jnp.where(pos, small, big)
    f = lb + (1.0 - lb) * sig
    lf_ref[...] = jnp.log2(f)
    rk_ref[...] = ((1.0 - lb) * nsig).astype(BF16)
    rv_ref[...] = cols(5).astype(BF16)
    sg_ref[...] = _silu(cols(6)).astype(BF16)


def _proj_call(x1, g, w_in, qg, kg, lbp, seg):
    m = x1.shape[0]
    half = lambda dt: jax.ShapeDtypeStruct((m, ATTN_WIDTH), dt)
    out_dtypes = [BF16, BF16, BF16, BF16, BF16, F32, BF16, BF16]
    return pl.pallas_call(
        _proj_kernel,
        grid=(m // ROW_TILE,),
        in_specs=[_rows(D_MODEL), _resident((1, D_MODEL)), _resident((D_MODEL, PROJ_COLS)),
                  _resident((1, ATTN_WIDTH)), _resident((1, ATTN_WIDTH)),
                  _resident(lbp.shape), _resident((MXU_TILE, MXU_TILE))],
        out_specs=[_rows(ATTN_WIDTH)] * 8,
        out_shape=[half(dt) for dt in out_dtypes],
        compiler_params=pltpu.CompilerParams(
            dimension_semantics=("arbitrary",), vmem_limit_bytes=VMEM_LIMIT),
        name="proj",
    )(x1, g, w_in, qg, kg, lbp, seg)


def _attention_stages(cg, q_ref, bias_ref, o_ref, kpad_ref, vt_ref, ones_ref):
    n_pairs = ATTN_HEADS // 2
    groups = MIX_ROWS // ATTN_TQ
    lane = lax.broadcasted_iota(jnp.int32, (ATTN_TQ, LANES), 1)
    even = lane < ATTN_HEAD_DIM

    def where(u):
        g, p = divmod(u, n_pairs)
        group = cg * groups + g
        window = pl.ds(pl.multiple_of(group * ATTN_TQ, ATTN_TQ), ATTN_WIN)
        return group, window, slice(g * ATTN_TQ, (g + 1) * ATTN_TQ), slice(p * LANES, (p + 1) * LANES), p

    def scores(u):
        group, window, rows, cols, p = where(u)
        qp = q_ref[0, rows, cols]
        zero = jnp.zeros_like(qp)
        q2 = jnp.concatenate([jnp.where(even, qp, zero), jnp.where(even, zero, qp)], axis=0)
        st = _dot_nt(kpad_ref[window, cols], q2) + bias_ref[p]
        first_valid = PAD // ATTN_TQ - group
        return jnp.concatenate(
            [jnp.where(j >= first_valid, st[j * ATTN_TQ:(j + 1) * ATTN_TQ], -jnp.inf)
             for j in range(ATTN_WIN // ATTN_TQ)], axis=0)

    def finish(u, st):
        _, window, rows, cols, _ = where(u)
        e = jnp.exp2(st - jnp.max(st, axis=0, keepdims=True)).astype(BF16)
        vt1 = jnp.concatenate([vt_ref[cols, window], ones_ref[:, window]], axis=0)
        ot = _dot(vt1, e)
        ot = ot[:LANES] / ot[LANES:LANES + 1]
        same_head = jnp.concatenate([ot[:ATTN_HEAD_DIM, :ATTN_TQ], ot[ATTN_HEAD_DIM:, ATTN_TQ:]], axis=0)
        o_ref[0, rows, cols] = same_head.T.astype(BF16)

    return groups * n_pairs, scores, finish


def _level_sum_matrix():
    u = np.arange(CHUNK)[:, None]
    r = np.arange(CHUNK)[None, :]
    mats = [(r <= u)]
    for lv in range(N_LEVELS):
        c = CHUNK >> (lv + 1)
        m = (u // (2 * c)) * (2 * c) + c
        upper = (u & c) != 0
        mats.append(np.where(upper, (r > m) & (r <= u), (r > u) & (r <= m)))
    return np.concatenate(mats, axis=0).astype(np.float32)


def _hgrn_stages(rq_ref, rk_ref, lf_ref, rv_ref, sg_ref, lsum_ref, og_ref, o_ref, state_ref):
    t_row = lax.broadcasted_iota(jnp.int32, (CHUNK, HGRN_HEAD_DIM), 0)
    t_sq = lax.broadcasted_iota(jnp.int32, (CHUNK, CHUNK), 0)
    s_sq = lax.broadcasted_iota(jnp.int32, (CHUNK, CHUNK), 1)
    tx = t_sq ^ s_sq
    esums = {}

    def where(n):
        u, hd = divmod(n, HGRN_HEADS)
        return u, slice(u * CHUNK, (u + 1) * CHUNK), slice(hd * HGRN_HEAD_DIM, (hd + 1) * HGRN_HEAD_DIM), hd

    def level_operands(q, k, w, c):
        if c >= SUBLANES:
            qs, ks = [], []
            zero = jnp.zeros((c, HGRN_HEAD_DIM), F32)
            for blk in range(CHUNK // c):
                r = slice(blk * c, (blk + 1) * c)
                qs.append(q[r] * w[r] if blk % 2 else zero)
                ks.append(zero if blk % 2 else k[r] * w[r])
            return jnp.concatenate(qs, axis=0).astype(BF16), jnp.concatenate(ks, axis=0).astype(BF16)
        upper = (t_row & c) != 0
        return (jnp.where(upper, q * w, 0.0).astype(BF16), jnp.where(upper, 0.0, k * w).astype(BF16))

    def products(n):
        u, rows, cols, _ = where(n)
        if u not in esums:
            esums[u] = _dot(lsum_ref[...], lf_ref[0, rows, :].astype(BF16))
        esum = esums[u]
        q = rq_ref[0, rows, cols].astype(F32)
        k = rk_ref[0, rows, cols].astype(F32)
        b = esum[:CHUNK, cols]
        b_last = b[CHUNK - 1:CHUNK, :]
        prods = []
        for lv in range(N_LEVELS):
            c = CHUNK >> (lv + 1)
            w = jnp.exp2(esum[(lv + 1) * CHUNK:(lv + 2) * CHUNK, cols])
            qw, kw = level_operands(q, k, w, c)
            prods.append(_dot_nt(qw, kw))
        return dict(q_in=(q * jnp.exp2(b)).astype(BF16), k_up=(k * jnp.exp2(b_last - b)).astype(BF16),
                    decay=jnp.exp2(b_last), diag=jnp.sum(q * k, axis=-1, keepdims=True), prods=prods)

    def finish(n, a):
        _, rows, cols, hd = where(n)
        scores = jnp.where(t_sq == s_sq, a["diag"], 0.0)
        for lv in range(N_LEVELS):
            scores = scores + jnp.where(tx < 2 * (CHUNK >> (lv + 1)), a["prods"][lv], 0.0)
        v = rv_ref[0, rows, cols]
        state = state_ref[hd]
        o = _dot_nt(a["q_in"], state.astype(BF16)) + _dot(scores.astype(BF16), v)
        state_ref[hd] = state * a["decay"] + _dot_tn(v, a["k_up"])
        y = _rms(o, og_ref[...]) * sg_ref[0, rows, cols].astype(F32)
        out_cols = slice(ATTN_WIDTH + hd * HGRN_HEAD_DIM, ATTN_WIDTH + (hd + 1) * HGRN_HEAD_DIM)
        o_ref[0, rows, out_cols] = y.astype(BF16)

    return (MIX_ROWS // CHUNK) * HGRN_HEADS, products, finish


def _mixer_kernel(q_ref, k_ref, v_ref, bias_ref, rq_ref, rk_ref, lf_ref, rv_ref, sg_ref, lsum_ref,
                  og_ref, o_ref, kpad_ref, vt_ref, ones_ref, state_ref):
    cg = pl.program_id(1)
    seq = k_ref.shape[1]

    @pl.when(cg == 0)
    def _():
        kpad_ref[:PAD, :] = jnp.zeros((PAD, ATTN_WIDTH), BF16)
        kpad_ref[PAD:, :] = k_ref[0]
        vt_ref[:, :PAD] = jnp.zeros((ATTN_WIDTH, PAD), BF16)
        for r in range(0, seq, MXU_TILE):
            vt_ref[:, PAD + r:PAD + r + MXU_TILE] = v_ref[0, r:r + MXU_TILE, :].T
        ones_ref[:, :PAD] = jnp.zeros((BF16_ROWS, PAD), BF16)
        ones_ref[:, PAD:] = jnp.ones((BF16_ROWS, seq), BF16)
        state_ref[...] = jnp.zeros_like(state_ref)

    n_attn, attn_first, attn_second = _attention_stages(cg, q_ref, bias_ref, o_ref, kpad_ref, vt_ref, ones_ref)
    n_hgrn, hgrn_first, hgrn_second = _hgrn_stages(rq_ref, rk_ref, lf_ref, rv_ref, sg_ref, lsum_ref,
                                                    og_ref, o_ref, state_ref)
    per = n_hgrn // n_attn
    a_next, h_next = attn_first(0), hgrn_first(0)
    for i in range(n_attn):
        a_cur, a_next = a_next, (attn_first(i + 1) if i + 1 < n_attn else None)
        attn_second(i, a_cur)
        for n in range(i * per, (i + 1) * per):
            h_cur, h_next = h_next, (hgrn_first(n + 1) if n + 1 < n_hgrn else None)
            hgrn_second(n, h_cur)


def _mixer_call(q, k, v, bias, rq, rk, lf, rv, sg, lsum, og):
    b, s, _ = q.shape
    blk = pl.BlockSpec((1, MIX_ROWS, ATTN_WIDTH), lambda i, j: (i, j, 0))
    whole = pl.BlockSpec((1, s, ATTN_WIDTH), lambda i, j: (i, 0, 0))
    return pl.pallas_call(
        _mixer_kernel,
        grid=(b, s // MIX_ROWS),
        in_specs=[blk, whole, whole, _resident(bias.shape)] + [blk] * 5
                 + [_resident(lsum.shape), _resident((1, HGRN_HEAD_DIM))],
        out_specs=pl.BlockSpec((1, MIX_ROWS, D_MODEL), lambda i, j: (i, j, 0)),
        out_shape=jax.ShapeDtypeStruct((b, s, D_MODEL), BF16),
        scratch_shapes=[pltpu.VMEM((s + PAD, ATTN_WIDTH), BF16),
                        pltpu.VMEM((ATTN_WIDTH, s + PAD), BF16),
                        pltpu.VMEM((BF16_ROWS, s + PAD), BF16),
                        pltpu.VMEM((HGRN_HEADS, HGRN_HEAD_DIM, HGRN_HEAD_DIM), F32)],
        compiler_params=pltpu.CompilerParams(
            dimension_semantics=("arbitrary", "arbitrary"), vmem_limit_bytes=VMEM_LIMIT),
        name="mixer",
    )(q, k, v, bias, rq, rk, lf, rv, sg, lsum, og)


def _rel_bias_table(rel_bias):
    assert ATTN_TQ - 1 <= REL_CLIP
    rb = rel_bias.astype(F32) * LOG2E
    n = ATTN_WIN + ATTN_TQ
    far = jnp.broadcast_to(rb[:, 2 * REL_CLIP:], (ATTN_HEADS, ATTN_WIN - REL_CLIP))
    near = rb[:, REL_CLIP - ATTN_TQ + 1:2 * REL_CLIP][:, ::-1]
    x = jnp.concatenate([far, near, jnp.zeros((ATTN_HEADS, 1), F32)], axis=1)
    rolled = jnp.tile(x, (1, ATTN_TQ))[:, :ATTN_TQ * (n - 1)].reshape(ATTN_HEADS, ATTN_TQ, n - 1)
    tbl = rolled[:, :, ATTN_TQ - 1:ATTN_TQ - 1 + ATTN_WIN]
    qc = np.arange(ATTN_TQ)[:, None] // CHUNK
    kc = np.arange(ATTN_WIN)[None, :] // CHUNK
    tbl = jnp.where((kc >= qc) & (kc <= qc + LEFT_CHUNKS), tbl, -jnp.inf)
    tbl = tbl.reshape(ATTN_HEADS // 2, 2, ATTN_TQ, ATTN_WIN).transpose(0, 3, 1, 2)
    return tbl.reshape(ATTN_HEADS // 2, ATTN_WIN, 2 * ATTN_TQ)


def kernel(x, ffn1_norm_g, ffn1_w_gate, ffn1_w_up, ffn1_w_down, mix_norm_g, w_in,
           attn_q_norm_g, attn_k_norm_g, attn_rel_bias, hgrn_lower_bounds, hgrn_out_norm_g,
           w_out, ffn2_norm_g, ffn2_w_gate, ffn2_w_up, ffn2_w_down):
    bsz, seq, _ = x.shape
    depth = ffn1_norm_g.shape[0]
    assert depth == 1 and seq % MIX_ROWS == 0 and (bsz * seq) % ROW_TILE == 0
    head_of_col = np.arange(MXU_TILE) // ATTN_HEAD_DIM
    seg = jnp.asarray(head_of_col[:, None] == head_of_col[None, :], dtype=BF16)
    lsum = jnp.asarray(_level_sum_matrix(), dtype=BF16)
    row = lambda g: g.reshape(1, -1).astype(F32)
    tile_heads = lambda g: jnp.tile(g.astype(F32), ATTN_HEADS).reshape(1, ATTN_WIDTH)

    xf = x.reshape(bsz * seq, D_MODEL)
    for l in range(depth):
        x1 = _ffn_call(xf, row(ffn1_norm_g[l]), ffn1_w_gate[l].astype(BF16),
                       ffn1_w_up[l].astype(BF16), ffn1_w_down[l].astype(BF16))
        parts = _proj_call(
            x1, row(mix_norm_g[l]), w_in[l].astype(BF16), tile_heads(attn_q_norm_g[l]),
            tile_heads(attn_k_norm_g[l]), hgrn_lower_bounds.astype(F32), seg)
        q, k, v, rq, rk, lf, rv, sg = [p.reshape(bsz, seq, ATTN_WIDTH) for p in parts]
        mixed = _mixer_call(q, k, v, _rel_bias_table(attn_rel_bias[l]), rq, rk, lf, rv, sg,
                            lsum, row(hgrn_out_norm_g[l]))
        xf = _ffn_call(x1, row(ffn2_norm_g[l]), ffn2_w_gate[l].astype(BF16),
                       ffn2_w_up[l].astype(BF16), ffn2_w_down[l].astype(BF16),
                       mix=(mixed.reshape(bsz * seq, D_MODEL), w_out[l].astype(BF16)))
    return xf.reshape(bsz, seq, D_MODEL)
```

```python
import numpy as np
import jax
import jax.numpy as jnp
from jax import lax
from jax.experimental import pallas as pl
from jax.experimental.pallas import tpu as pltpu

D_MODEL = 1024
CHUNK = 64
ATTN_WIDTH = 512
HGRN_WIDTH = 512
ATTN_HEAD_DIM = 64
ATTN_HEADS = 8
HGRN_HEAD_DIM = 128
HGRN_HEADS = 4
LEFT_CHUNKS = 8
PAD = LEFT_CHUNKS * CHUNK
REL_CLIP = 128
D_FF = 2816
RMS_EPS = 1e-6
PROJ_COLS = 3 * ATTN_WIDTH + 4 * HGRN_WIDTH

LANES = 128
SUBLANES = 8
BF16_ROWS = 2 * SUBLANES
MXU_TILE = 256
FF_TILE = MXU_TILE
ROW_TILE = 512
LOG2E = float(np.log2(np.e))
ATTN_Q_CHUNKS = 2
ATTN_TQ = ATTN_Q_CHUNKS * CHUNK
ATTN_WIN = PAD + ATTN_TQ
MIX_ROWS = 512
N_LEVELS = 6
LEVELS = tuple(CHUNK >> (lv + 1) for lv in range(N_LEVELS))
FINE_LEVELS = tuple(c for c in LEVELS if c < SUBLANES)
VMEM_LIMIT = 56 * 1024 * 1024

BF16 = jnp.bfloat16
F32 = jnp.float32


def _dot(a, b):
    return jnp.dot(a, b, preferred_element_type=F32)


def _dot_nt(a, b):
    return lax.dot_general(a, b, (((1,), (1,)), ((), ())), preferred_element_type=F32)


def _dot_tn(a, b):
    return lax.dot_general(a, b, (((0,), (0,)), ((), ())), preferred_element_type=F32)


def _silu(x):
    return x / (1.0 + jnp.exp(-x))


def _rms(x, g):
    ms = jnp.mean(x * x, axis=-1, keepdims=True)
    return x * lax.rsqrt(ms + RMS_EPS) * g


def _ffn_body(x, g_ref, wg_ref, wu_ref, wd_ref, o_ref, act_ref):
    h = _rms(x, g_ref[...]).astype(BF16)
    for j in range(D_FF // FF_TILE):
        cols = slice(j * FF_TILE, (j + 1) * FF_TILE)
        gate = _dot(h, wg_ref[:, cols])
        up = _dot(h, wu_ref[:, cols])
        act_ref[:, cols] = (_silu(gate) * up).astype(BF16)
    y = _dot(act_ref[...], wd_ref[...])
    o_ref[...] = x + 0.5 * y


def _ffn_kernel(x_ref, g_ref, wg_ref, wu_ref, wd_ref, o_ref, act_ref):
    _ffn_body(x_ref[...], g_ref, wg_ref, wu_ref, wd_ref, o_ref, act_ref)


def _mix_ffn_kernel(x_ref, m_ref, wo_ref, g_ref, wg_ref, wu_ref, wd_ref, o_ref, act_ref):
    x = x_ref[...] + _dot(m_ref[...], wo_ref[...])
    _ffn_body(x, g_ref, wg_ref, wu_ref, wd_ref, o_ref, act_ref)


def _resident(shape):
    return pl.BlockSpec(shape, lambda *_: (0,) * len(shape), pipeline_mode=pl.Buffered(1))


def _rows(width):
    return pl.BlockSpec((ROW_TILE, width), lambda i: (i, 0))


def _ffn_call(x, g, wg, wu, wd, mix=None):
    m = x.shape[0]
    w_specs = [_resident((1, D_MODEL)), _resident((D_MODEL, D_FF)),
               _resident((D_MODEL, D_FF)), _resident((D_FF, D_MODEL))]
    if mix is None:
        kern, ins, specs = _ffn_kernel, (x, g, wg, wu, wd), [_rows(D_MODEL)] + w_specs
    else:
        mixed, wo = mix
        kern = _mix_ffn_kernel
        ins = (x, mixed, wo, g, wg, wu, wd)
        specs = [_rows(D_MODEL), _rows(D_MODEL), _resident((D_MODEL, D_MODEL))] + w_specs
    return pl.pallas_call(
        kern,
        grid=(m // ROW_TILE,),
        in_specs=specs,
        out_specs=_rows(D_MODEL),
        out_shape=jax.ShapeDtypeStruct((m, D_MODEL), F32),
        scratch_shapes=[pltpu.VMEM((ROW_TILE, D_FF), BF16)],
        compiler_params=pltpu.CompilerParams(
            dimension_semantics=("arbitrary",), vmem_limit_bytes=VMEM_LIMIT),
        name="mix_ffn" if mix is not None else "ffn",
    )(*ins)


def _head_mean_square(a, seg_ref):
    sq = (a * a).astype(BF16)
    sums = [_dot(sq[:, j:j + MXU_TILE], seg_ref[...]) for j in range(0, ATTN_WIDTH, MXU_TILE)]
    return jnp.concatenate(sums, axis=1) * (1.0 / ATTN_HEAD_DIM)


def _proj_kernel(x_ref, g_ref, w_ref, qg_ref, kg_ref, lbp_ref, seg_ref,
                 q_ref, k_ref, v_ref, rq_ref, rk_ref, lf_ref, rv_ref, sg_ref):
    h = _rms(x_ref[...], g_ref[...]).astype(BF16)

    def cols(i):
        return _dot(h, w_ref[:, i * ATTN_WIDTH:(i + 1) * ATTN_WIDTH])

    def put_q(aq):
        scale = ATTN_HEAD_DIM ** -0.5 * LOG2E
        q_ref[...] = (aq * lax.rsqrt(_head_mean_square(aq, seg_ref) + RMS_EPS)
                      * (qg_ref[...] * scale)).astype(BF16)

    def put_k(ak):
        k_ref[...] = (ak * lax.rsqrt(_head_mean_square(ak, seg_ref) + RMS_EPS)
                      * kg_ref[...]).astype(BF16)

    def put_v(av):
        v_ref[...] = av.astype(BF16)

    def put_rq(hq):
        rq_ref[...] = _silu(hq).astype(BF16)

    def put_forget(hf):
        lbp = lbp_ref[...]
        e = jnp.exp(lbp - jnp.max(lbp, axis=0, keepdims=True))
        lb = e[0:1, :] / jnp.sum(e, axis=0, keepdims=True)
        en = jnp.exp(-jnp.abs(hf))
        big, small = 1.0 / (1.0 + en), en / (1.0 + en)
        pos = hf >= 0
        sig = jnp.where(pos, big, small)
        nsig = jnp.where(pos, small, big)
        f = lb + (1.0 - lb) * sig
        lf_ref[...] = jnp.log2(f)
        rk_ref[...] = ((1.0 - lb) * nsig).astype(BF16)

    def put_rv(hi):
        rv_ref[...] = hi.astype(BF16)

    def put_sg(hg):
        sg_ref[...] = _silu(hg).astype(BF16)

    tails = [put_q, put_k, put_v, put_rq, put_forget, put_rv, put_sg]
    nxt = cols(0)
    for i, tail in enumerate(tails):
        cur, nxt = nxt, (cols(i + 1) if i + 1 < len(tails) else None)
        tail(cur)


def _proj_call(x1, g, w_in, qg, kg, lbp, seg):
    m = x1.shape[0]
    half = lambda dt: jax.ShapeDtypeStruct((m, ATTN_WIDTH), dt)
    out_dtypes = [BF16, BF16, BF16, BF16, BF16, F32, BF16, BF16]
    return pl.pallas_call(
        _proj_kernel,
        grid=(m // ROW_TILE,),
        in_specs=[_rows(D_MODEL), _resident((1, D_MODEL)), _resident((D_MODEL, PROJ_COLS)),
                  _resident((1, ATTN_WIDTH)), _resident((1, ATTN_WIDTH)),
                  _resident(lbp.shape), _resident((MXU_TILE, MXU_TILE))],
        out_specs=[_rows(ATTN_WIDTH)] * 8,
        out_shape=[half(dt) for dt in out_dtypes],
        compiler_params=pltpu.CompilerParams(
            dimension_semantics=("arbitrary",), vmem_limit_bytes=VMEM_LIMIT),
        name="proj",
    )(x1, g, w_in, qg, kg, lbp, seg)


def _attention_stages(cg, q_ref, bias_ref, o_ref, kpad_ref, vt_ref, ones_ref):
    n_pairs = ATTN_HEADS // 2
    groups = MIX_ROWS // ATTN_TQ
    lane = lax.broadcasted_iota(jnp.int32, (ATTN_TQ, LANES), 1)
    even = lane < ATTN_HEAD_DIM

    def where(u):
        g, p = divmod(u, n_pairs)
        group = cg * groups + g
        window = pl.ds(pl.multiple_of(group * ATTN_TQ, ATTN_TQ), ATTN_WIN)
        return group, window, slice(g * ATTN_TQ, (g + 1) * ATTN_TQ), slice(p * LANES, (p + 1) * LANES), p

    def scores(u):
        group, window, rows, cols, p = where(u)
        qp = q_ref[0, rows, cols]
        zero = jnp.zeros_like(qp)
        q2 = jnp.concatenate([jnp.where(even, qp, zero), jnp.where(even, zero, qp)], axis=0)
        st = _dot_nt(kpad_ref[window, cols], q2)
        first_valid = PAD // ATTN_TQ - group
        blocks = []
        for j in range(ATTN_WIN // ATTN_TQ):
            sj = st[j * ATTN_TQ:(j + 1) * ATTN_TQ]
            if j in BIAS_BLOCKS:
                sj = sj + bias_ref[p, BIAS_BLOCKS.index(j)]
            blocks.append(jnp.where(j >= first_valid, sj, -jnp.inf))
        return jnp.concatenate(blocks, axis=0)

    def finish(u, st):
        _, window, rows, cols, _ = where(u)
        e = jnp.exp2(st - jnp.max(st, axis=0, keepdims=True)).astype(BF16)
        vt1 = jnp.concatenate([vt_ref[cols, window], ones_ref[:, window]], axis=0)
        ot = _dot(vt1, e)
        ot = ot[:LANES] / ot[LANES:LANES + 1]
        same_head = jnp.concatenate([ot[:ATTN_HEAD_DIM, :ATTN_TQ], ot[ATTN_HEAD_DIM:, ATTN_TQ:]], axis=0)
        o_ref[0, rows, cols] = same_head.T.astype(BF16)

    return groups * n_pairs, scores, finish


def _level_sum_matrix():
    u = np.arange(CHUNK)[:, None]
    r = np.arange(CHUNK)[None, :]
    mats = [(r <= u)]
    for c in FINE_LEVELS:
        m = (u // (2 * c)) * (2 * c) + c
        upper = (u & c) != 0
        mats.append(np.where(upper, (r > m) & (r <= u), (r > u) & (r <= m)))
    return np.concatenate(mats, axis=0).astype(np.float32)


def _level_masks():
    t = np.arange(CHUNK)[:, None]
    s = np.arange(CHUNK)[None, :]
    return np.stack([t == s] + [(t ^ s) < 2 * c for c in LEVELS]).astype(np.float32)


def _hgrn_stages(rq_ref, rk_ref, lf_ref, rv_ref, sg_ref, lsum_ref, lmask_ref, og_ref, o_ref, state_ref):
    t_row = lax.broadcasted_iota(jnp.int32, (CHUNK, HGRN_HEAD_DIM), 0)
    esums = {}

    def where(n):
        u, hd = divmod(n, HGRN_HEADS)
        return u, slice(u * CHUNK, (u + 1) * CHUNK), slice(hd * HGRN_HEAD_DIM, (hd + 1) * HGRN_HEAD_DIM), hd

    def coarse_operands(q, k, b, c):
        qs, ks = [], []
        zero = jnp.zeros((c, HGRN_HEAD_DIM), F32)
        for blk in range(CHUNK // c):
            r = slice(blk * c, (blk + 1) * c)
            m = (blk // 2) * 2 * c + c
            b_m = b[m:m + 1, :]
            qs.append(q[r] * jnp.exp2(b[r] - b_m) if blk % 2 else zero)
            ks.append(zero if blk % 2 else k[r] * jnp.exp2(b_m - b[r]))
        return jnp.concatenate(qs, axis=0).astype(BF16), jnp.concatenate(ks, axis=0).astype(BF16)

    def fine_operands(q, k, w, c):
        upper = (t_row & c) != 0
        return (jnp.where(upper, q * w, 0.0).astype(BF16), jnp.where(upper, 0.0, k * w).astype(BF16))

    def products(n):
        u, rows, cols, _ = where(n)
        if u not in esums:
            esums[u] = _dot(lsum_ref[...], lf_ref[0, rows, :].astype(BF16))
        esum = esums[u]
        q = rq_ref[0, rows, cols].astype(F32)
        k = rk_ref[0, rows, cols].astype(F32)
        b = esum[:CHUNK, cols]
        b_last = b[CHUNK - 1:CHUNK, :]
        prods = []
        for c in LEVELS:
            if c in FINE_LEVELS:
                i = 1 + FINE_LEVELS.index(c)
                qw, kw = fine_operands(q, k, jnp.exp2(esum[i * CHUNK:(i + 1) * CHUNK, cols]), c)
            else:
                qw, kw = coarse_operands(q, k, b, c)
            prods.append(_dot_nt(qw, kw))
        return dict(q_in=(q * jnp.exp2(b)).astype(BF16), k_up=(k * jnp.exp2(b_last - b)).astype(BF16),
                    decay=jnp.exp2(b_last), diag=jnp.sum(q * k, axis=-1, keepdims=True), prods=prods)

    def finish(n, a):
        _, rows, cols, hd = where(n)
        scores = a["diag"] * lmask_ref[0]
        for lv in range(N_LEVELS):
            scores = scores + a["prods"][lv] * lmask_ref[1 + lv]
        v = rv_ref[0, rows, cols]
        state = state_ref[hd]
        o = _dot_nt(a["q_in"], state.astype(BF16)) + _dot(scores.astype(BF16), v)
        state_ref[hd] = state * a["decay"] + _dot_tn(v, a["k_up"])
        y = _rms(o, og_ref[...]) * sg_ref[0, rows, cols].astype(F32)
        out_cols = slice(ATTN_WIDTH + hd * HGRN_HEAD_DIM, ATTN_WIDTH + (hd + 1) * HGRN_HEAD_DIM)
        o_ref[0, rows, out_cols] = y.astype(BF16)

    return (MIX_ROWS // CHUNK) * HGRN_HEADS, products, finish


def _mixer_kernel(q_ref, k_ref, v_ref, bias_ref, rq_ref, rk_ref, lf_ref, rv_ref, sg_ref, lsum_ref,
                  lmask_ref, og_ref, o_ref, kpad_ref, vt_ref, ones_ref, state_ref):
    cg = pl.program_id(1)
    seq = k_ref.shape[1]

    @pl.when(cg == 0)
    def _():
        kpad_ref[:PAD, :] = jnp.zeros((PAD, ATTN_WIDTH), BF16)
        kpad_ref[PAD:, :] = k_ref[0]
        vt_ref[:, :PAD] = jnp.zeros((ATTN_WIDTH, PAD), BF16)
        for r in range(0, seq, MXU_TILE):
            vt_ref[:, PAD + r:PAD + r + MXU_TILE] = v_ref[0, r:r + MXU_TILE, :].T
        ones_ref[:, :PAD] = jnp.zeros((BF16_ROWS, PAD), BF16)
        ones_ref[:, PAD:] = jnp.ones((BF16_ROWS, seq), BF16)
        state_ref[...] = jnp.zeros_like(state_ref)

    n_attn, attn_first, attn_second = _attention_stages(cg, q_ref, bias_ref, o_ref, kpad_ref, vt_ref, ones_ref)
    n_hgrn, hgrn_first, hgrn_second = _hgrn_stages(rq_ref, rk_ref, lf_ref, rv_ref, sg_ref, lsum_ref,
                                                    lmask_ref, og_ref, o_ref, state_ref)
    per = n_hgrn // n_attn
    a_next, h_next = attn_first(0), hgrn_first(0)
    for i in range(n_attn):
        a_cur, a_next = a_next, (attn_first(i + 1) if i + 1 < n_attn else None)
        for n in range(i * per, (i + 1) * per):
            h_cur, h_next = h_next, (hgrn_first(n + 1) if n + 1 < n_hgrn else None)
            hgrn_second(n, h_cur)
            if n == i * per:
                attn_second(i, a_cur)


def _mixer_call(q, k, v, bias, rq, rk, lf, rv, sg, lsum, lmask, og):
    b, s, _ = q.shape
    blk = pl.BlockSpec((1, MIX_ROWS, ATTN_WIDTH), lambda i, j: (i, j, 0))
    whole = pl.BlockSpec((1, s, ATTN_WIDTH), lambda i, j: (i, 0, 0))
    return pl.pallas_call(
        _mixer_kernel,
        grid=(b, s // MIX_ROWS),
        in_specs=[blk, whole, whole, _resident(bias.shape)] + [blk] * 5
                 + [_resident(lsum.shape), _resident(lmask.shape), _resident((1, HGRN_HEAD_DIM))],
        out_specs=pl.BlockSpec((1, MIX_ROWS, D_MODEL), lambda i, j: (i, j, 0)),
        out_shape=jax.ShapeDtypeStruct((b, s, D_MODEL), BF16),
        scratch_shapes=[pltpu.VMEM((s + PAD, ATTN_WIDTH), BF16),
                        pltpu.VMEM((ATTN_WIDTH, s + PAD), BF16),
                        pltpu.VMEM((BF16_ROWS, s + PAD), BF16),
                        pltpu.VMEM((HGRN_HEADS, HGRN_HEAD_DIM, HGRN_HEAD_DIM), F32)],
        compiler_params=pltpu.CompilerParams(
            dimension_semantics=("arbitrary", "arbitrary"), vmem_limit_bytes=VMEM_LIMIT),
        name="mixer",
    )(q, k, v, bias, rq, rk, lf, rv, sg, lsum, lmask, og)


def _rel_bias_table(rel_bias):
    assert ATTN_TQ - 1 <= REL_CLIP
    rb = (rel_bias.astype(F32) - rel_bias.astype(F32)[:, 2 * REL_CLIP:]) * LOG2E
    n = ATTN_WIN + ATTN_TQ
    far = jnp.broadcast_to(rb[:, 2 * REL_CLIP:], (ATTN_HEADS, ATTN_WIN - REL_CLIP))
    near = rb[:, REL_CLIP - ATTN_TQ + 1:2 * REL_CLIP][:, ::-1]
    x = jnp.concatenate([far, near, jnp.zeros((ATTN_HEADS, 1), F32)], axis=1)
    rolled = jnp.tile(x, (1, ATTN_TQ))[:, :ATTN_TQ * (n - 1)].reshape(ATTN_HEADS, ATTN_TQ, n - 1)
    tbl = rolled[:, :, ATTN_TQ - 1:ATTN_TQ - 1 + ATTN_WIN]
    qc = np.arange(ATTN_TQ)[:, None] // CHUNK
    kc = np.arange(ATTN_WIN)[None, :] // CHUNK
    tbl = jnp.where((kc >= qc) & (kc <= qc + LEFT_CHUNKS), tbl, -jnp.inf)
    tbl = tbl.reshape(ATTN_HEADS // 2, 2, ATTN_TQ, ATTN_WIN).transpose(0, 3, 1, 2)
    tbl = tbl.reshape(ATTN_HEADS // 2, ATTN_WIN // ATTN_TQ, ATTN_TQ, 2 * ATTN_TQ)
    return tbl[:, np.asarray(BIAS_BLOCKS)]


def _bias_blocks():
    qq = np.arange(ATTN_TQ)[:, None]
    key = np.arange(ATTN_WIN)[None, :]
    near = PAD + qq - key < REL_CLIP
    band = (key // CHUNK >= qq // CHUNK) & (key // CHUNK <= qq // CHUNK + LEFT_CHUNKS)
    used = (near | ~band).any(axis=0).reshape(ATTN_WIN // ATTN_TQ, ATTN_TQ).any(axis=1)
    return tuple(int(j) for j in np.nonzero(used)[0])


BIAS_BLOCKS = _bias_blocks()


def kernel(x, ffn1_norm_g, ffn1_w_gate, ffn1_w_up, ffn1_w_down, mix_norm_g, w_in,
           attn_q_norm_g, attn_k_norm_g, attn_rel_bias, hgrn_lower_bounds, hgrn_out_norm_g,
           w_out, ffn2_norm_g, ffn2_w_gate, ffn2_w_up, ffn2_w_down):
    bsz, seq, _ = x.shape
    depth = ffn1_norm_g.shape[0]
    assert depth == 1 and seq % MIX_ROWS == 0 and (bsz * seq) % ROW_TILE == 0
    head_of_col = np.arange(MXU_TILE) // ATTN_HEAD_DIM
    seg = jnp.asarray(head_of_col[:, None] == head_of_col[None, :], dtype=BF16)
    lsum = jnp.asarray(_level_sum_matrix(), dtype=BF16)
    lmask = jnp.asarray(_level_masks())
    row = lambda g: g.reshape(1, -1).astype(F32)
    tile_heads = lambda g: jnp.tile(g.astype(F32), ATTN_HEADS).reshape(1, ATTN_WIDTH)

    xf = x.reshape(bsz * seq, D_MODEL)
    for l in range(depth):
        x1 = _ffn_call(xf, row(ffn1_norm_g[l]), ffn1_w_gate[l].astype(BF16),
                       ffn1_w_up[l].astype(BF16), ffn1_w_down[l].astype(BF16))
        parts = _proj_call(
            x1, row(mix_norm_g[l]), w_in[l].astype(BF16), tile_heads(attn_q_norm_g[l]),
            tile_heads(attn_k_norm_g[l]), hgrn_lower_bounds.astype(F32), seg)
        q, k, v, rq, rk, lf, rv, sg = [p.reshape(bsz, seq, ATTN_WIDTH) for p in parts]
        mixed = _mixer_call(q, k, v, _rel_bias_table(attn_rel_bias[l]), rq, rk, lf, rv, sg,
                            lsum, lmask, row(hgrn_out_norm_g[l]))
        xf = _ffn_call(x1, row(ffn2_norm_g[l]), ffn2_w_gate[l].astype(BF16),
                       ffn2_w_up[l].astype(BF16), ffn2_w_down[l].astype(BF16),
                       mix=(mixed.reshape(bsz * seq, D_MODEL), w_out[l].astype(BF16)))
    return xf.reshape(bsz, seq, D_MODEL)
```

```python
import numpy as np
import jax
import jax.numpy as jnp
from jax import lax
from jax.experimental import pallas as pl
from jax.experimental.pallas import tpu as pltpu

D_MODEL = 1024
CHUNK = 64
ATTN_WIDTH = 512
HGRN_WIDTH = 512
ATTN_HEAD_DIM = 64
ATTN_HEADS = 8
HGRN_HEAD_DIM = 128
HGRN_HEADS = 4
LEFT_CHUNKS = 8
PAD = LEFT_CHUNKS * CHUNK
REL_CLIP = 128
D_FF = 2816
RMS_EPS = 1e-6
PROJ_GROUPS = 7
PROJ_COLS = PROJ_GROUPS * ATTN_WIDTH

LANES = 128
SUBLANES = 8
BF16_ROWS = 2 * SUBLANES
MXU_TILE = 256
FF_TILE = MXU_TILE
ROW_TILE = 1024
LOG2E = float(np.log2(np.e))
ATTN_Q_CHUNKS = 2
ATTN_TQ = ATTN_Q_CHUNKS * CHUNK
ATTN_WIN = PAD + ATTN_TQ
MIX_ROWS = 512
N_LEVELS = 6
LEVELS = tuple(CHUNK >> (lv + 1) for lv in range(N_LEVELS))
FINE_LEVELS = tuple(c for c in LEVELS if c < SUBLANES)
VMEM_LIMIT = 56 * 1024 * 1024

BF16 = jnp.bfloat16
F32 = jnp.float32


def _dot(a, b):
    return jnp.dot(a, b, preferred_element_type=F32)


def _dot_nt(a, b):
    return lax.dot_general(a, b, (((1,), (1,)), ((), ())), preferred_element_type=F32)


def _dot_tn(a, b):
    return lax.dot_general(a, b, (((0,), (0,)), ((), ())), preferred_element_type=F32)


def _silu(x):
    return x / (1.0 + jnp.exp(-x))


def _rms(x, g):
    ms = jnp.mean(x * x, axis=-1, keepdims=True)
    return x * lax.rsqrt(ms + RMS_EPS) * g


def _resident(shape):
    return pl.BlockSpec(shape, lambda *_: (0,) * len(shape), pipeline_mode=pl.Buffered(1))


def _ffn_body(x, g_ref, wg_ref, wu_ref, wd_ref, o_ref, act_ref):
    h = _rms(x, g_ref[...]).astype(BF16)
    for j in range(D_FF // FF_TILE):
        cols = slice(j * FF_TILE, (j + 1) * FF_TILE)
        gate = _dot(h, wg_ref[:, cols])
        up = _dot(h, wu_ref[:, cols])
        act_ref[:, cols] = (_silu(gate) * up).astype(BF16)
    y = _dot(act_ref[...], wd_ref[...])
    o_ref[...] = x + 0.5 * y


def _ffn_kernel(x_ref, g_ref, wg_ref, wu_ref, wd_ref, o_ref, act_ref):
    _ffn_body(x_ref[...], g_ref, wg_ref, wu_ref, wd_ref, o_ref, act_ref)


def _mix_ffn_kernel(x_ref, m_ref, wo_ref, g_ref, wg_ref, wu_ref, wd_ref, o_ref, act_ref):
    x = x_ref[...] + _dot(m_ref[...], wo_ref[...])
    _ffn_body(x, g_ref, wg_ref, wu_ref, wd_ref, o_ref, act_ref)


def _rows(width):
    return pl.BlockSpec((ROW_TILE, width), lambda i: (i, 0))


def _ffn_call(x, g, wg, wu, wd, mix=None):
    m = x.shape[0]
    w_specs = [_resident((1, D_MODEL)), _resident((D_MODEL, D_FF)),
               _resident((D_MODEL, D_FF)), _resident((D_FF, D_MODEL))]
    if mix is None:
        kern, ins, specs = _ffn_kernel, (x, g, wg, wu, wd), [_rows(D_MODEL)] + w_specs
    else:
        mixed, wo = mix
        kern = _mix_ffn_kernel
        ins = (x, mixed, wo, g, wg, wu, wd)
        specs = [_rows(D_MODEL), _rows(D_MODEL), _resident((D_MODEL, D_MODEL))] + w_specs
    return pl.pallas_call(
        kern,
        grid=(m // ROW_TILE,),
        in_specs=specs,
        out_specs=_rows(D_MODEL),
        out_shape=jax.ShapeDtypeStruct((m, D_MODEL), F32),
        scratch_shapes=[pltpu.VMEM((ROW_TILE, D_FF), BF16)],
        compiler_params=pltpu.CompilerParams(
            dimension_semantics=("arbitrary",), vmem_limit_bytes=VMEM_LIMIT),
        name="mix_ffn" if mix is not None else "ffn",
    )(*ins)


def _projection_stages(x_ref, g_ref, w_ref, qg_ref, kg_ref, lbp_ref, seg_ref, h_ref, dst):
    names = ("q", "k", "v", "rq", "forget", "rv", "sg")
    per_group = ATTN_WIDTH // MXU_TILE

    def normalize():
        h_ref[...] = _rms(x_ref[0], g_ref[...]).astype(BF16)

    def project(i):
        return _dot(h_ref[...], w_ref[:, i * MXU_TILE:(i + 1) * MXU_TILE])

    def head_normed(a, gain):
        ms = _dot((a * a).astype(BF16), seg_ref[...]) * (1.0 / ATTN_HEAD_DIM)
        return (a * lax.rsqrt(ms + RMS_EPS) * gain).astype(BF16)

    def tail(i, val):
        group, part = divmod(i, per_group)
        cols = slice(part * MXU_TILE, (part + 1) * MXU_TILE)
        name = names[group]
        if name == "q":
            dst["q"](cols, head_normed(val, qg_ref[:, cols] * (ATTN_HEAD_DIM ** -0.5 * LOG2E)))
        elif name == "k":
            dst["k"](cols, head_normed(val, kg_ref[:, cols]))
        elif name in ("v", "rv"):
            dst[name](cols, val.astype(BF16))
        elif name in ("rq", "sg"):
            dst[name](cols, _silu(val).astype(BF16))
        else:
            lbp = lbp_ref[:, cols]
            e = jnp.exp(lbp - jnp.max(lbp, axis=0, keepdims=True))
            lb = e[0:1, :] / jnp.sum(e, axis=0, keepdims=True)
            en = jnp.exp(-jnp.abs(val))
            big, small = 1.0 / (1.0 + en), en / (1.0 + en)
            pos = val >= 0
            f = lb + (1.0 - lb) * jnp.where(pos, big, small)
            dst["lf"](cols, jnp.log2(f))
            dst["rk"](cols, ((1.0 - lb) * jnp.where(pos, small, big)).astype(BF16))

    return PROJ_GROUPS * per_group, normalize, (project, tail)


def _attention_stages(tile, src, bias_ref, o_ref):
    n_pairs = ATTN_HEADS // 2
    groups = MIX_ROWS // ATTN_TQ
    lane = lax.broadcasted_iota(jnp.int32, (ATTN_TQ, LANES), 1)
    even = lane < ATTN_HEAD_DIM

    def where(u):
        g, p = divmod(u, n_pairs)
        group = tile * groups + g
        window = pl.ds(pl.multiple_of(group * ATTN_TQ, ATTN_TQ), ATTN_WIN)
        return group, window, slice(g * ATTN_TQ, (g + 1) * ATTN_TQ), slice(p * LANES, (p + 1) * LANES), p

    def scores(u):
        group, window, rows, cols, p = where(u)
        qp = src["q"](rows, cols)
        zero = jnp.zeros_like(qp)
        q2 = jnp.concatenate([jnp.where(even, qp, zero), jnp.where(even, zero, qp)], axis=0)
        st = _dot_nt(src["k"](window, cols), q2)
        first_valid = PAD // ATTN_TQ - group
        blocks = []
        for j in range(ATTN_WIN // ATTN_TQ):
            sj = st[j * ATTN_TQ:(j + 1) * ATTN_TQ]
            if j in BIAS_BLOCKS:
                sj = sj + bias_ref[p, BIAS_BLOCKS.index(j)]
            blocks.append(jnp.where(j >= first_valid, sj, -jnp.inf))
        return jnp.concatenate(blocks, axis=0)

    def weigh(u, st):
        _, window, rows, cols, _ = where(u)
        e = jnp.exp2(st - jnp.max(st, axis=0, keepdims=True)).astype(BF16)
        vt1 = jnp.concatenate([src["vt"](cols, window), src["ones"](window)], axis=0)
        ot = _dot(vt1, e)
        ot = ot[:LANES] / ot[LANES:LANES + 1]
        same_head = jnp.concatenate([ot[:ATTN_HEAD_DIM, :ATTN_TQ], ot[ATTN_HEAD_DIM:, ATTN_TQ:]], axis=0)
        o_ref[0, rows, cols] = same_head.T.astype(BF16)

    return groups * n_pairs, (scores, weigh)


def _level_sum_matrix():
    u = np.arange(CHUNK)[:, None]
    r = np.arange(CHUNK)[None, :]
    mats = [(r <= u)]
    for c in FINE_LEVELS:
        m = (u // (2 * c)) * (2 * c) + c
        upper = (u & c) != 0
        mats.append(np.where(upper, (r > m) & (r <= u), (r > u) & (r <= m)))
    return np.concatenate(mats, axis=0).astype(np.float32)


def _level_masks():
    t = np.arange(CHUNK)[:, None]
    s = np.arange(CHUNK)[None, :]
    return np.stack([t == s] + [(t ^ s) < 2 * c for c in LEVELS]).astype(np.float32)


def _hgrn_stages(src, lsum_ref, lmask_ref, og_ref, o_ref, state_ref):
    t_row = lax.broadcasted_iota(jnp.int32, (CHUNK, HGRN_HEAD_DIM), 0)
    esums = {}

    def where(n):
        u, hd = divmod(n, HGRN_HEADS)
        return u, slice(u * CHUNK, (u + 1) * CHUNK), slice(hd * HGRN_HEAD_DIM, (hd + 1) * HGRN_HEAD_DIM), hd

    def coarse_operands(q, k, b, c):
        qs, ks = [], []
        zero = jnp.zeros((c, HGRN_HEAD_DIM), F32)
        for blk in range(CHUNK // c):
            r = slice(blk * c, (blk + 1) * c)
            m = (blk // 2) * 2 * c + c
            b_m = b[m:m + 1, :]
            qs.append(q[r] * jnp.exp2(b[r] - b_m) if blk % 2 else zero)
            ks.append(zero if blk % 2 else k[r] * jnp.exp2(b_m - b[r]))
        return jnp.concatenate(qs, axis=0).astype(BF16), jnp.concatenate(ks, axis=0).astype(BF16)

    def fine_operands(q, k, w, c):
        upper = (t_row & c) != 0
        return (jnp.where(upper, q * w, 0.0).astype(BF16), jnp.where(upper, 0.0, k * w).astype(BF16))

    def products(n):
        u, rows, cols, _ = where(n)
        if u not in esums:
            esums[u] = _dot(lsum_ref[...], src["lf"](rows, slice(None)).astype(BF16))
        esum = esums[u]
        q = src["rq"](rows, cols).astype(F32)
        k = src["rk"](rows, cols).astype(F32)
        b = esum[:CHUNK, cols]
        b_last = b[CHUNK - 1:CHUNK, :]
        prods = []
        for c in LEVELS:
            if c in FINE_LEVELS:
                i = 1 + FINE_LEVELS.index(c)
                qw, kw = fine_operands(q, k, jnp.exp2(esum[i * CHUNK:(i + 1) * CHUNK, cols]), c)
            else:
                qw, kw = coarse_operands(q, k, b, c)
            prods.append(_dot_nt(qw, kw))
        return dict(q_in=(q * jnp.exp2(b)).astype(BF16), k_up=(k * jnp.exp2(b_last - b)).astype(BF16),
                    decay=jnp.exp2(b_last), diag=jnp.sum(q * k, axis=-1, keepdims=True), prods=prods)

    def finish(n, a):
        _, rows, cols, hd = where(n)
        scores = a["diag"] * lmask_ref[0]
        for lv in range(N_LEVELS):
            scores = scores + a["prods"][lv] * lmask_ref[1 + lv]
        v = src["rv"](rows, cols)
        state = state_ref[hd]
        o = _dot_nt(a["q_in"], state.astype(BF16)) + _dot(scores.astype(BF16), v)
        state_ref[hd] = state * a["decay"] + _dot_tn(v, a["k_up"])
        y = _rms(o, og_ref[...]) * src["sg"](rows, cols).astype(F32)
        out_cols = slice(ATTN_WIDTH + hd * HGRN_HEAD_DIM, ATTN_WIDTH + (hd + 1) * HGRN_HEAD_DIM)
        o_ref[0, rows, out_cols] = y.astype(BF16)

    return (MIX_ROWS // CHUNK) * HGRN_HEADS, (products, finish)


def _mixer_kernel(x0_ref, xn_ref, g_ref, w_ref, qg_ref, kg_ref, lbp_ref, seg_ref, bias_ref, lsum_ref,
                  lmask_ref, og_ref, o_ref, h_ref, q_ref, rq_ref, rk_ref, rv_ref, sg_ref, lf_ref,
                  kpad_ref, vt_ref, ones_ref, state_ref):
    seq = kpad_ref.shape[1] - PAD
    tiles = seq // MIX_ROWS
    s = pl.program_id(0)
    tile = lax.rem(s, tiles)
    nxt = jnp.minimum(s + 1, pl.num_programs(0) - 1)

    def projection(x_ref, step):
        slot = lax.rem(step, 2)
        par = lax.rem(lax.div(step, tiles), 2)
        rows = pl.ds(pl.multiple_of(PAD + lax.rem(step, tiles) * MIX_ROWS, MIX_ROWS), MIX_ROWS)

        def put(ref):
            def store(cols, val):
                ref[slot, :, cols] = val
            return store

        def put_k(cols, val):
            kpad_ref[par, rows, cols] = val

        def put_v(cols, val):
            vt_ref[par, cols, rows] = val.T

        dst = dict(q=put(q_ref), k=put_k, v=put_v, rq=put(rq_ref), rk=put(rk_ref), rv=put(rv_ref),
                   sg=put(sg_ref), lf=put(lf_ref))
        return _projection_stages(x_ref, g_ref, w_ref, qg_ref, kg_ref, lbp_ref, seg_ref, h_ref, dst)

    @pl.when(s == 0)
    def _():
        kpad_ref[:, :PAD, :] = jnp.zeros((2, PAD, ATTN_WIDTH), BF16)
        vt_ref[:, :, :PAD] = jnp.zeros((2, ATTN_WIDTH, PAD), BF16)
        ones_ref[:, :PAD] = jnp.zeros((BF16_ROWS, PAD), BF16)
        ones_ref[:, PAD:] = jnp.ones((BF16_ROWS, seq), BF16)
        n_units, normalize, (project, tail) = projection(x0_ref, s)
        normalize()
        nxt_val = project(0)
        for i in range(n_units):
            cur, nxt_val = nxt_val, (project(i + 1) if i + 1 < n_units else None)
            tail(i, cur)

    @pl.when(tile == 0)
    def _():
        state_ref[...] = jnp.zeros_like(state_ref)

    slot = lax.rem(s, 2)
    par = lax.rem(lax.div(s, tiles), 2)
    tile_src = lambda ref: (lambda rows, cols: ref[slot, rows, cols])
    src = dict(q=tile_src(q_ref), rq=tile_src(rq_ref), rk=tile_src(rk_ref), rv=tile_src(rv_ref),
               sg=tile_src(sg_ref), lf=tile_src(lf_ref),
               k=lambda window, cols: kpad_ref[par, window, cols],
               vt=lambda cols, window: vt_ref[par, cols, window],
               ones=lambda window: ones_ref[:, window])
    n_proj, normalize, (proj_first, proj_second) = projection(xn_ref, nxt)
    n_attn, (attn_first, attn_second) = _attention_stages(tile, src, bias_ref, o_ref)
    n_hgrn, (hgrn_first, hgrn_second) = _hgrn_stages(src, lsum_ref, lmask_ref, og_ref, o_ref, state_ref)

    per = n_hgrn // n_attn
    normalize()
    p_next, a_next, h_next = proj_first(0), attn_first(0), hgrn_first(0)
    for i in range(n_attn):
        a_cur, a_next = a_next, (attn_first(i + 1) if i + 1 < n_attn else None)
        for n in range(i * per, (i + 1) * per):
            h_cur, h_next = h_next, (hgrn_first(n + 1) if n + 1 < n_hgrn else None)
            hgrn_second(n, h_cur)
            if n == i * per:
                attn_second(i, a_cur)
                if i < n_proj:
                    p_cur, p_next = p_next, (proj_first(i + 1) if i + 1 < n_proj else None)
                    proj_second(i, p_cur)


def _mixer_call(x1, g, w_in, qg, kg, lbp, seg, bias, lsum, lmask, og):
    b, s, _ = x1.shape
    tiles = s // MIX_ROWS
    steps = b * tiles
    slots = lambda dt: pltpu.VMEM((2, MIX_ROWS, ATTN_WIDTH), dt)

    def next_tile(i):
        n = jnp.minimum(i + 1, steps - 1)
        return (n // tiles, n % tiles, 0)

    return pl.pallas_call(
        _mixer_kernel,
        grid=(steps,),
        in_specs=[pl.BlockSpec((1, MIX_ROWS, D_MODEL), lambda i: (0, 0, 0), pipeline_mode=pl.Buffered(1)),
                  pl.BlockSpec((1, MIX_ROWS, D_MODEL), next_tile),
                  _resident((1, D_MODEL)), _resident((D_MODEL, PROJ_COLS)),
                  _resident((1, ATTN_WIDTH)), _resident((1, ATTN_WIDTH)), _resident(lbp.shape),
                  _resident((MXU_TILE, MXU_TILE)), _resident(bias.shape), _resident(lsum.shape),
                  _resident(lmask.shape), _resident((1, HGRN_HEAD_DIM))],
        out_specs=pl.BlockSpec((1, MIX_ROWS, D_MODEL), lambda i: (i // tiles, i % tiles, 0)),
        out_shape=jax.ShapeDtypeStruct((b, s, D_MODEL), BF16),
        scratch_shapes=[pltpu.VMEM((MIX_ROWS, D_MODEL), BF16),
                        slots(BF16), slots(BF16), slots(BF16), slots(BF16), slots(BF16), slots(F32),
                        pltpu.VMEM((2, s + PAD, ATTN_WIDTH), BF16),
                        pltpu.VMEM((2, ATTN_WIDTH, s + PAD), BF16),
                        pltpu.VMEM((BF16_ROWS, s + PAD), BF16),
                        pltpu.VMEM((HGRN_HEADS, HGRN_HEAD_DIM, HGRN_HEAD_DIM), F32)],
        compiler_params=pltpu.CompilerParams(
            dimension_semantics=("arbitrary",), vmem_limit_bytes=VMEM_LIMIT),
        name="mixer",
    )(x1, x1, g, w_in, qg, kg, lbp, seg, bias, lsum, lmask, og)


def _bias_blocks():
    qq = np.arange(ATTN_TQ)[:, None]
    key = np.arange(ATTN_WIN)[None, :]
    near = PAD + qq - key < REL_CLIP
    band = (key // CHUNK >= qq // CHUNK) & (key // CHUNK <= qq // CHUNK + LEFT_CHUNKS)
    used = (near | ~band).any(axis=0).reshape(ATTN_WIN // ATTN_TQ, ATTN_TQ).any(axis=1)
    return tuple(int(j) for j in np.nonzero(used)[0])


BIAS_BLOCKS = _bias_blocks()


def _rel_bias_table(rel_bias):
    assert ATTN_TQ - 1 <= REL_CLIP
    rb = (rel_bias.astype(F32) - rel_bias.astype(F32)[:, 2 * REL_CLIP:]) * LOG2E
    n = ATTN_WIN + ATTN_TQ
    far = jnp.broadcast_to(rb[:, 2 * REL_CLIP:], (ATTN_HEADS, ATTN_WIN - REL_CLIP))
    near = rb[:, REL_CLIP - ATTN_TQ + 1:2 * REL_CLIP][:, ::-1]
    x = jnp.concatenate([far, near, jnp.zeros((ATTN_HEADS, 1), F32)], axis=1)
    rolled = jnp.tile(x, (1, ATTN_TQ))[:, :ATTN_TQ * (n - 1)].reshape(ATTN_HEADS, ATTN_TQ, n - 1)
    tbl = rolled[:, :, ATTN_TQ - 1:ATTN_TQ - 1 + ATTN_WIN]
    qc = np.arange(ATTN_TQ)[:, None] // CHUNK
    kc = np.arange(ATTN_WIN)[None, :] // CHUNK
    tbl = jnp.where((kc >= qc) & (kc <= qc + LEFT_CHUNKS), tbl, -jnp.inf)
    tbl = tbl.reshape(ATTN_HEADS // 2, 2, ATTN_TQ, ATTN_WIN).transpose(0, 3, 1, 2)
    tbl = tbl.reshape(ATTN_HEADS // 2, ATTN_WIN // ATTN_TQ, ATTN_TQ, 2 * ATTN_TQ)
    return tbl[:, np.asarray(BIAS_BLOCKS)]


def kernel(x, ffn1_norm_g, ffn1_w_gate, ffn1_w_up, ffn1_w_down, mix_norm_g, w_in,
           attn_q_norm_g, attn_k_norm_g, attn_rel_bias, hgrn_lower_bounds, hgrn_out_norm_g,
           w_out, ffn2_norm_g, ffn2_w_gate, ffn2_w_up, ffn2_w_down):
    bsz, seq, _ = x.shape
    depth = ffn1_norm_g.shape[0]
    assert depth == 1 and seq % MIX_ROWS == 0 and (bsz * seq) % ROW_TILE == 0
    head_of_col = np.arange(MXU_TILE) // ATTN_HEAD_DIM
    seg = jnp.asarray(head_of_col[:, None] == head_of_col[None, :], dtype=BF16)
    lsum = jnp.asarray(_level_sum_matrix(), dtype=BF16)
    lmask = jnp.asarray(_level_masks())
    row = lambda g: g.reshape(1, -1).astype(F32)
    tile_heads = lambda g: jnp.tile(g.astype(F32), ATTN_HEADS).reshape(1, ATTN_WIDTH)

    xf = x.reshape(bsz * seq, D_MODEL)
    for l in range(depth):
        x1 = _ffn_call(xf, row(ffn1_norm_g[l]), ffn1_w_gate[l].astype(BF16),
                       ffn1_w_up[l].astype(BF16), ffn1_w_down[l].astype(BF16))
        mixed = _mixer_call(
            x1.reshape(bsz, seq, D_MODEL), row(mix_norm_g[l]), w_in[l].astype(BF16),
            tile_heads(attn_q_norm_g[l]), tile_heads(attn_k_norm_g[l]), hgrn_lower_bounds.astype(F32),
            seg, _rel_bias_table(attn_rel_bias[l]), lsum, lmask, row(hgrn_out_norm_g[l]))
        xf = _ffn_call(x1, row(ffn2_norm_g[l]), ffn2_w_gate[l].astype(BF16),
                       ffn2_w_up[l].astype(BF16), ffn2_w_down[l].astype(BF16),
                       mix=(mixed.reshape(bsz * seq, D_MODEL), w_out[l].astype(BF16)))
    return xf.reshape(bsz, seq, D_MODEL)
```

```python
import functools

import numpy as np
import jax
import jax.numpy as jnp
from jax import lax
from jax.experimental import pallas as pl
from jax.experimental.pallas import tpu as pltpu

D_MODEL = 1024
CHUNK = 64
ATTN_WIDTH = 512
HGRN_WIDTH = 512
ATTN_HEAD_DIM = 64
ATTN_HEADS = 8
HGRN_HEAD_DIM = 128
HGRN_HEADS = 4
LEFT_CHUNKS = 8
PAD = LEFT_CHUNKS * CHUNK
REL_CLIP = 128
D_FF = 2816
RMS_EPS = 1e-6
PROJ_GROUPS = 7
PROJ_COLS = PROJ_GROUPS * ATTN_WIDTH

LANES = 128
SUBLANES = 8
BF16_ROWS = 2 * SUBLANES
MXU_TILE = 256
FF_TILE = MXU_TILE
ROW_TILE = 1024
LOG2E = float(np.log2(np.e))
ATTN_Q_CHUNKS = 2
ATTN_TQ = ATTN_Q_CHUNKS * CHUNK
ATTN_WIN = PAD + ATTN_TQ
MIX_ROWS = 512
N_LEVELS = 6
LEVELS = tuple(CHUNK >> (lv + 1) for lv in range(N_LEVELS))
FINE_LEVELS = tuple(c for c in LEVELS if c < SUBLANES)
VMEM_LIMIT = 56 * 1024 * 1024

BF16 = jnp.bfloat16
F32 = jnp.float32


def _dot(a, b):
    return jnp.dot(a, b, preferred_element_type=F32)


def _dot_nt(a, b):
    return lax.dot_general(a, b, (((1,), (1,)), ((), ())), preferred_element_type=F32)


def _dot_tn(a, b):
    return lax.dot_general(a, b, (((0,), (0,)), ((), ())), preferred_element_type=F32)


def _silu(x):
    return x / (1.0 + jnp.exp(-x))


def _rms(x, g):
    ms = jnp.mean(x * x, axis=-1, keepdims=True)
    return x * lax.rsqrt(ms + RMS_EPS) * g


def _resident(shape):
    return pl.BlockSpec(shape, lambda *_: (0,) * len(shape), pipeline_mode=pl.Buffered(1))


def _ffn_body(x, g_ref, wg_ref, wu_ref, wd_ref, o_ref, act_ref):
    h = _rms(x, g_ref[...]).astype(BF16)
    for j in range(D_FF // FF_TILE):
        cols = slice(j * FF_TILE, (j + 1) * FF_TILE)
        gate = _dot(h, wg_ref[:, cols])
        up = _dot(h, wu_ref[:, cols])
        act_ref[:, cols] = (_silu(gate) * up).astype(BF16)
    y = _dot(act_ref[...], wd_ref[...])
    o_ref[...] = x + 0.5 * y


def _ffn_kernel(n_cast, x_ref, g_ref, wg_ref, wu_ref, wd_ref, *rest):
    cast_in, (o_ref, *cast_out, act_ref) = rest[:n_cast], rest[n_cast:]
    for src, dst in zip(cast_in, cast_out):
        dst[...] = src[...].astype(BF16)
    _ffn_body(x_ref[...], g_ref, wg_ref, wu_ref, wd_ref, o_ref, act_ref)


def _mix_ffn_kernel(x_ref, m_ref, wo_ref, g_ref, wg_ref, wu_ref, wd_ref, o_ref, act_ref):
    x = x_ref[...] + _dot(m_ref[...], wo_ref[...])
    _ffn_body(x, g_ref, wg_ref, wu_ref, wd_ref, o_ref, act_ref)


def _rows(width):
    return pl.BlockSpec((ROW_TILE, width), lambda i: (i, 0))


def _slab_spec(shape, steps):
    rows, width = shape
    per = 1 if (rows // steps) % BF16_ROWS == 0 and rows % steps == 0 else 2
    slab = rows * per // steps
    assert slab * steps == rows * per and slab % BF16_ROWS == 0
    return pl.BlockSpec((slab, width), lambda i: (i // per, 0))


def _ffn_call(x, g, wg, wu, wd, mix=None, cast=()):
    m = x.shape[0]
    steps = m // ROW_TILE
    w_specs = [_resident((1, D_MODEL)), _resident((D_MODEL, D_FF)),
               _resident((D_MODEL, D_FF)), _resident((D_FF, D_MODEL))]
    out_specs, out_shape = [_rows(D_MODEL)], [jax.ShapeDtypeStruct((m, D_MODEL), F32)]
    if mix is None:
        kern = functools.partial(_ffn_kernel, len(cast))
        ins = (x, g, wg, wu, wd) + tuple(cast)
        specs = [_rows(D_MODEL)] + w_specs + [_slab_spec(w.shape, steps) for w in cast]
        out_specs += [_slab_spec(w.shape, steps) for w in cast]
        out_shape += [jax.ShapeDtypeStruct(w.shape, BF16) for w in cast]
    else:
        assert not cast
        mixed, wo = mix
        kern = _mix_ffn_kernel
        ins = (x, mixed, wo, g, wg, wu, wd)
        specs = [_rows(D_MODEL), _rows(D_MODEL), _resident((D_MODEL, D_MODEL))] + w_specs
    return pl.pallas_call(
        kern,
        grid=(steps,),
        in_specs=specs,
        out_specs=out_specs,
        out_shape=out_shape,
        scratch_shapes=[pltpu.VMEM((ROW_TILE, D_FF), BF16)],
        compiler_params=pltpu.CompilerParams(
            dimension_semantics=("arbitrary",), vmem_limit_bytes=VMEM_LIMIT),
        name="mix_ffn" if mix is not None else "ffn",
    )(*ins)


def _projection_stages(x_ref, g_ref, w_ref, qg_ref, kg_ref, lbp_ref, seg_ref, h_ref, dst):
    names = ("q", "k", "v", "rq", "forget", "rv", "sg")
    per_group = ATTN_WIDTH // MXU_TILE

    def normalize():
        h_ref[...] = _rms(x_ref[0], g_ref[...]).astype(BF16)

    def project(i):
        return _dot(h_ref[...], w_ref[:, i * MXU_TILE:(i + 1) * MXU_TILE])

    def head_normed(a, gain):
        ms = _dot((a * a).astype(BF16), seg_ref[...]) * (1.0 / ATTN_HEAD_DIM)
        return (a * lax.rsqrt(ms + RMS_EPS) * gain).astype(BF16)

    def tail(i, val):
        group, part = divmod(i, per_group)
        cols = slice(part * MXU_TILE, (part + 1) * MXU_TILE)
        name = names[group]
        if name == "q":
            dst["q"](cols, head_normed(val, qg_ref[:, cols] * (ATTN_HEAD_DIM ** -0.5 * LOG2E)))
        elif name == "k":
            dst["k"](cols, head_normed(val, kg_ref[:, cols]))
        elif name in ("v", "rv"):
            dst[name](cols, val.astype(BF16))
        elif name in ("rq", "sg"):
            dst[name](cols, _silu(val).astype(BF16))
        else:
            lbp = lbp_ref[:, cols]
            e = jnp.exp(lbp - jnp.max(lbp, axis=0, keepdims=True))
            lb = e[0:1, :] / jnp.sum(e, axis=0, keepdims=True)
            en = jnp.exp(-jnp.abs(val))
            big, small = 1.0 / (1.0 + en), en / (1.0 + en)
            pos = val >= 0
            f = lb + (1.0 - lb) * jnp.where(pos, big, small)
            dst["lf"](cols, jnp.log2(f))
            dst["rk"](cols, ((1.0 - lb) * jnp.where(pos, small, big)).astype(BF16))

    return PROJ_GROUPS * per_group, normalize, (project, tail)


def _attention_stages(tile, src, bias_ref, o_ref):
    n_pairs = ATTN_HEADS // 2
    groups = MIX_ROWS // ATTN_TQ
    lane = lax.broadcasted_iota(jnp.int32, (ATTN_TQ, LANES), 1)
    even = lane < ATTN_HEAD_DIM

    def where(u):
        g, p = divmod(u, n_pairs)
        group = tile * groups + g
        window = pl.ds(pl.multiple_of(group * ATTN_TQ, ATTN_TQ), ATTN_WIN)
        return group, window, slice(g * ATTN_TQ, (g + 1) * ATTN_TQ), slice(p * LANES, (p + 1) * LANES), p

    def scores(u):
        group, window, rows, cols, p = where(u)
        qp = src["q"](rows, cols)
        zero = jnp.zeros_like(qp)
        q2 = jnp.concatenate([jnp.where(even, qp, zero), jnp.where(even, zero, qp)], axis=0)
        st = _dot_nt(src["k"](window, cols), q2)
        first_valid = PAD // ATTN_TQ - group
        blocks = []
        for j in range(ATTN_WIN // ATTN_TQ):
            sj = st[j * ATTN_TQ:(j + 1) * ATTN_TQ]
            if j in BIAS_BLOCKS:
                sj = sj + bias_ref[p, BIAS_BLOCKS.index(j)]
            blocks.append(jnp.where(j >= first_valid, sj, -jnp.inf))
        return jnp.concatenate(blocks, axis=0)

    def weigh(u, st):
        _, window, rows, cols, _ = where(u)
        e = jnp.exp2(st - jnp.max(st, axis=0, keepdims=True)).astype(BF16)
        vt1 = jnp.concatenate([src["vt"](cols, window), src["ones"](window)], axis=0)
        ot = _dot(vt1, e)
        ot = ot[:LANES] / ot[LANES:LANES + 1]
        same_head = jnp.concatenate([ot[:ATTN_HEAD_DIM, :ATTN_TQ], ot[ATTN_HEAD_DIM:, ATTN_TQ:]], axis=0)
        o_ref[0, rows, cols] = same_head.T.astype(BF16)

    return groups * n_pairs, (scores, weigh)


def _level_sum_matrix():
    u = np.arange(CHUNK)[:, None]
    r = np.arange(CHUNK)[None, :]
    mats = [(r <= u)]
    for c in FINE_LEVELS:
        m = (u // (2 * c)) * (2 * c) + c
        upper = (u & c) != 0
        mats.append(np.where(upper, (r > m) & (r <= u), (r > u) & (r <= m)))
    return np.concatenate(mats, axis=0).astype(np.float32)


def _level_masks():
    t = np.arange(CHUNK)[:, None]
    s = np.arange(CHUNK)[None, :]
    return np.stack([t == s] + [(t ^ s) < 2 * c for c in LEVELS]).astype(np.float32)


def _hgrn_stages(src, lsum_ref, lmask_ref, og_ref, o_ref, state_ref):
    t_row = lax.broadcasted_iota(jnp.int32, (CHUNK, HGRN_HEAD_DIM), 0)
    esums = {}

    def where(n):
        u, hd = divmod(n, HGRN_HEADS)
        return u, slice(u * CHUNK, (u + 1) * CHUNK), slice(hd * HGRN_HEAD_DIM, (hd + 1) * HGRN_HEAD_DIM), hd

    def coarse_operands(q, k, b, c):
        qs, ks = [], []
        zero = jnp.zeros((c, HGRN_HEAD_DIM), F32)
        for blk in range(CHUNK // c):
            r = slice(blk * c, (blk + 1) * c)
            m = (blk // 2) * 2 * c + c
            b_m = b[m:m + 1, :]
            qs.append(q[r] * jnp.exp2(b[r] - b_m) if blk % 2 else zero)
            ks.append(zero if blk % 2 else k[r] * jnp.exp2(b_m - b[r]))
        return jnp.concatenate(qs, axis=0).astype(BF16), jnp.concatenate(ks, axis=0).astype(BF16)

    def fine_operands(q, k, w, c):
        upper = (t_row & c) != 0
        return (jnp.where(upper, q * w, 0.0).astype(BF16), jnp.where(upper, 0.0, k * w).astype(BF16))

    def products(n):
        u, rows, cols, _ = where(n)
        if u not in esums:
            esums[u] = _dot(lsum_ref[...], src["lf"](rows, slice(None)).astype(BF16))
        esum = esums[u]
        q = src["rq"](rows, cols).astype(F32)
        k = src["rk"](rows, cols).astype(F32)
        b = esum[:CHUNK, cols]
        b_last = b[CHUNK - 1:CHUNK, :]
        prods = []
        for c in LEVELS:
            if c in FINE_LEVELS:
                i = 1 + FINE_LEVELS.index(c)
                qw, kw = fine_operands(q, k, jnp.exp2(esum[i * CHUNK:(i + 1) * CHUNK, cols]), c)
            else:
                qw, kw = coarse_operands(q, k, b, c)
            prods.append(_dot_nt(qw, kw))
        return dict(q_in=(q * jnp.exp2(b)).astype(BF16), k_up=(k * jnp.exp2(b_last - b)).astype(BF16),
                    decay=jnp.exp2(b_last), diag=jnp.sum(q * k, axis=-1, keepdims=True), prods=prods)

    def finish(n, a):
        _, rows, cols, hd = where(n)
        scores = a["diag"] * lmask_ref[0]
        for lv in range(N_LEVELS):
            scores = scores + a["prods"][lv] * lmask_ref[1 + lv]
        v = src["rv"](rows, cols)
        state = state_ref[hd]
        o = _dot_nt(a["q_in"], state.astype(BF16)) + _dot(scores.astype(BF16), v)
        state_ref[hd] = state * a["decay"] + _dot_tn(v, a["k_up"])
        y = _rms(o, og_ref[...]) * src["sg"](rows, cols).astype(F32)
        out_cols = slice(ATTN_WIDTH + hd * HGRN_HEAD_DIM, ATTN_WIDTH + (hd + 1) * HGRN_HEAD_DIM)
        o_ref[0, rows, out_cols] = y.astype(BF16)

    return (MIX_ROWS // CHUNK) * HGRN_HEADS, (products, finish)


def _mixer_kernel(x0_ref, xn_ref, g_ref, w_ref, qg_ref, kg_ref, lbp_ref, seg_ref, bias_ref, lsum_ref,
                  lmask_ref, og_ref, o_ref, h_ref, q_ref, rq_ref, rk_ref, rv_ref, sg_ref, lf_ref,
                  kpad_ref, vt_ref, ones_ref, state_ref):
    seq = kpad_ref.shape[1] - PAD
    tiles = seq // MIX_ROWS
    s = pl.program_id(0)
    tile = lax.rem(s, tiles)
    nxt = jnp.minimum(s + 1, pl.num_programs(0) - 1)

    def projection(x_ref, step):
        slot = lax.rem(step, 2)
        par = lax.rem(lax.div(step, tiles), 2)
        rows = pl.ds(pl.multiple_of(PAD + lax.rem(step, tiles) * MIX_ROWS, MIX_ROWS), MIX_ROWS)

        def put(ref):
            def store(cols, val):
                ref[slot, :, cols] = val
            return store

        def put_k(cols, val):
            kpad_ref[par, rows, cols] = val

        def put_v(cols, val):
            vt_ref[par, cols, rows] = val.T

        dst = dict(q=put(q_ref), k=put_k, v=put_v, rq=put(rq_ref), rk=put(rk_ref), rv=put(rv_ref),
                   sg=put(sg_ref), lf=put(lf_ref))
        return _projection_stages(x_ref, g_ref, w_ref, qg_ref, kg_ref, lbp_ref, seg_ref, h_ref, dst)

    @pl.when(s == 0)
    def _():
        kpad_ref[:, :PAD, :] = jnp.zeros((2, PAD, ATTN_WIDTH), BF16)
        vt_ref[:, :, :PAD] = jnp.zeros((2, ATTN_WIDTH, PAD), BF16)
        ones_ref[:, :PAD] = jnp.zeros((BF16_ROWS, PAD), BF16)
        ones_ref[:, PAD:] = jnp.ones((BF16_ROWS, seq), BF16)
        n_units, normalize, (project, tail) = projection(x0_ref, s)
        normalize()
        nxt_val = project(0)
        for i in range(n_units):
            cur, nxt_val = nxt_val, (project(i + 1) if i + 1 < n_units else None)
            tail(i, cur)

    @pl.when(tile == 0)
    def _():
        state_ref[...] = jnp.zeros_like(state_ref)

    slot = lax.rem(s, 2)
    par = lax.rem(lax.div(s, tiles), 2)
    tile_src = lambda ref: (lambda rows, cols: ref[slot, rows, cols])
    src = dict(q=tile_src(q_ref), rq=tile_src(rq_ref), rk=tile_src(rk_ref), rv=tile_src(rv_ref),
               sg=tile_src(sg_ref), lf=tile_src(lf_ref),
               k=lambda window, cols: kpad_ref[par, window, cols],
               vt=lambda cols, window: vt_ref[par, cols, window],
               ones=lambda window: ones_ref[:, window])
    n_proj, normalize, (proj_first, proj_second) = projection(xn_ref, nxt)
    n_attn, (attn_first, attn_second) = _attention_stages(tile, src, bias_ref, o_ref)
    n_hgrn, (hgrn_first, hgrn_second) = _hgrn_stages(src, lsum_ref, lmask_ref, og_ref, o_ref, state_ref)

    per = n_hgrn // n_attn
    normalize()
    p_next, a_next, h_next = proj_first(0), attn_first(0), hgrn_first(0)
    for i in range(n_attn):
        a_cur, a_next = a_next, (attn_first(i + 1) if i + 1 < n_attn else None)
        for n in range(i * per, (i + 1) * per):
            h_cur, h_next = h_next, (hgrn_first(n + 1) if n + 1 < n_hgrn else None)
            hgrn_second(n, h_cur)
            if n == i * per:
                attn_second(i, a_cur)
                if i < n_proj:
                    p_cur, p_next = p_next, (proj_first(i + 1) if i + 1 < n_proj else None)
                    proj_second(i, p_cur)


def _mixer_call(x1, g, w_in, qg, kg, lbp, seg, bias, lsum, lmask, og):
    b, s, _ = x1.shape
    tiles = s // MIX_ROWS
    steps = b * tiles
    slots = lambda dt: pltpu.VMEM((2, MIX_ROWS, ATTN_WIDTH), dt)

    def next_tile(i):
        n = jnp.minimum(i + 1, steps - 1)
        return (n // tiles, n % tiles, 0)

    return pl.pallas_call(
        _mixer_kernel,
        grid=(steps,),
        in_specs=[pl.BlockSpec((1, MIX_ROWS, D_MODEL), lambda i: (0, 0, 0), pipeline_mode=pl.Buffered(1)),
                  pl.BlockSpec((1, MIX_ROWS, D_MODEL), next_tile),
                  _resident((1, D_MODEL)), _resident((D_MODEL, PROJ_COLS)),
                  _resident((1, ATTN_WIDTH)), _resident((1, ATTN_WIDTH)), _resident(lbp.shape),
                  _resident((MXU_TILE, MXU_TILE)), _resident(bias.shape), _resident(lsum.shape),
                  _resident(lmask.shape), _resident((1, HGRN_HEAD_DIM))],
        out_specs=pl.BlockSpec((1, MIX_ROWS, D_MODEL), lambda i: (i // tiles, i % tiles, 0)),
        out_shape=jax.ShapeDtypeStruct((b, s, D_MODEL), BF16),
        scratch_shapes=[pltpu.VMEM((MIX_ROWS, D_MODEL), BF16),
                        slots(BF16), slots(BF16), slots(BF16), slots(BF16), slots(BF16), slots(F32),
                        pltpu.VMEM((2, s + PAD, ATTN_WIDTH), BF16),
                        pltpu.VMEM((2, ATTN_WIDTH, s + PAD), BF16),
                        pltpu.VMEM((BF16_ROWS, s + PAD), BF16),
                        pltpu.VMEM((HGRN_HEADS, HGRN_HEAD_DIM, HGRN_HEAD_DIM), F32)],
        compiler_params=pltpu.CompilerParams(
            dimension_semantics=("arbitrary",), vmem_limit_bytes=VMEM_LIMIT),
        name="mixer",
    )(x1, x1, g, w_in, qg, kg, lbp, seg, bias, lsum, lmask, og)


def _bias_blocks():
    qq = np.arange(ATTN_TQ)[:, None]
    key = np.arange(ATTN_WIN)[None, :]
    near = PAD + qq - key < REL_CLIP
    band = (key // CHUNK >= qq // CHUNK) & (key // CHUNK <= qq // CHUNK + LEFT_CHUNKS)
    used = (near | ~band).any(axis=0).reshape(ATTN_WIN // ATTN_TQ, ATTN_TQ).any(axis=1)
    return tuple(int(j) for j in np.nonzero(used)[0])


BIAS_BLOCKS = _bias_blocks()


def _rel_bias_table(rel_bias):
    assert ATTN_TQ - 1 <= REL_CLIP
    rb = (rel_bias.astype(F32) - rel_bias.astype(F32)[:, 2 * REL_CLIP:]) * LOG2E
    n = ATTN_WIN + ATTN_TQ
    far = jnp.broadcast_to(rb[:, 2 * REL_CLIP:], (ATTN_HEADS, ATTN_WIN - REL_CLIP))
    near = rb[:, REL_CLIP - ATTN_TQ + 1:2 * REL_CLIP][:, ::-1]
    x = jnp.concatenate([far, near, jnp.zeros((ATTN_HEADS, 1), F32)], axis=1)
    rolled = jnp.tile(x, (1, ATTN_TQ))[:, :ATTN_TQ * (n - 1)].reshape(ATTN_HEADS, ATTN_TQ, n - 1)
    tbl = rolled[:, :, ATTN_TQ - 1:ATTN_TQ - 1 + ATTN_WIN]
    qc = np.arange(ATTN_TQ)[:, None] // CHUNK
    kc = np.arange(ATTN_WIN)[None, :] // CHUNK
    tbl = jnp.where((kc >= qc) & (kc <= qc + LEFT_CHUNKS), tbl, -jnp.inf)
    tbl = tbl.reshape(ATTN_HEADS // 2, 2, ATTN_TQ, ATTN_WIN).transpose(0, 3, 1, 2)
    tbl = tbl.reshape(ATTN_HEADS // 2, ATTN_WIN // ATTN_TQ, ATTN_TQ, 2 * ATTN_TQ)
    return tbl[:, np.asarray(BIAS_BLOCKS)]


def kernel(x, ffn1_norm_g, ffn1_w_gate, ffn1_w_up, ffn1_w_down, mix_norm_g, w_in,
           attn_q_norm_g, attn_k_norm_g, attn_rel_bias, hgrn_lower_bounds, hgrn_out_norm_g,
           w_out, ffn2_norm_g, ffn2_w_gate, ffn2_w_up, ffn2_w_down):
    bsz, seq, _ = x.shape
    depth = ffn1_norm_g.shape[0]
    assert depth == 1 and seq % MIX_ROWS == 0 and (bsz * seq) % ROW_TILE == 0
    head_of_col = np.arange(MXU_TILE) // ATTN_HEAD_DIM
    seg = jnp.asarray(head_of_col[:, None] == head_of_col[None, :], dtype=BF16)
    lsum = jnp.asarray(_level_sum_matrix(), dtype=BF16)
    lmask = jnp.asarray(_level_masks())
    row = lambda g: g.reshape(1, -1).astype(F32)
    tile_heads = lambda g: jnp.tile(g.astype(F32), ATTN_HEADS).reshape(1, ATTN_WIDTH)

    xf = x.reshape(bsz * seq, D_MODEL)
    for l in range(depth):
        later = (w_in[l], w_out[l], ffn2_w_gate[l], ffn2_w_up[l], ffn2_w_down[l])
        x1, w_in_b, w_out_b, wg2_b, wu2_b, wd2_b = _ffn_call(
            xf, row(ffn1_norm_g[l]), ffn1_w_gate[l].astype(BF16), ffn1_w_up[l].astype(BF16),
            ffn1_w_down[l].astype(BF16), cast=tuple(w.astype(F32) for w in later))
        mixed = _mixer_call(
            x1.reshape(bsz, seq, D_MODEL), row(mix_norm_g[l]), w_in_b,
            tile_heads(attn_q_norm_g[l]), tile_heads(attn_k_norm_g[l]), hgrn_lower_bounds.astype(F32),
            seg, _rel_bias_table(attn_rel_bias[l]), lsum, lmask, row(hgrn_out_norm_g[l]))
        xf, = _ffn_call(x1, row(ffn2_norm_g[l]), wg2_b, wu2_b, wd2_b,
                        mix=(mixed.reshape(bsz * seq, D_MODEL), w_out_b))
    return xf.reshape(bsz, seq, D_MODEL)
```

```python
import functools

import numpy as np
import jax
import jax.numpy as jnp
from jax import lax
from jax.experimental import pallas as pl
from jax.experimental.pallas import tpu as pltpu

D_MODEL = 1024
CHUNK = 64
ATTN_WIDTH = 512
HGRN_WIDTH = 512
ATTN_HEAD_DIM = 64
ATTN_HEADS = 8
HGRN_HEAD_DIM = 128
HGRN_HEADS = 4
LEFT_CHUNKS = 8
PAD = LEFT_CHUNKS * CHUNK
REL_CLIP = 128
D_FF = 2816
RMS_EPS = 1e-6
PROJ_GROUPS = 7
PROJ_COLS = PROJ_GROUPS * ATTN_WIDTH

LANES = 128
SUBLANES = 8
BF16_ROWS = 2 * SUBLANES
MXU_TILE = 256
FF_TILE = MXU_TILE
ROW_TILE = 1024
LOG2E = float(np.log2(np.e))
ATTN_Q_CHUNKS = 2
ATTN_TQ = ATTN_Q_CHUNKS * CHUNK
ATTN_WIN = PAD + ATTN_TQ
MIX_ROWS = 512
N_LEVELS = 6
LEVELS = tuple(CHUNK >> (lv + 1) for lv in range(N_LEVELS))
FINE_LEVELS = tuple(c for c in LEVELS if c < SUBLANES)
VMEM_LIMIT = 56 * 1024 * 1024

BF16 = jnp.bfloat16
F32 = jnp.float32


def _dot(a, b):
    return jnp.dot(a, b, preferred_element_type=F32)


def _dot_nt(a, b):
    return lax.dot_general(a, b, (((1,), (1,)), ((), ())), preferred_element_type=F32)


def _dot_tn(a, b):
    return lax.dot_general(a, b, (((0,), (0,)), ((), ())), preferred_element_type=F32)


def _silu(x):
    return x / (1.0 + jnp.exp(-x))


def _rms(x, g):
    ms = jnp.mean(x * x, axis=-1, keepdims=True)
    return x * lax.rsqrt(ms + RMS_EPS) * g


def _resident(shape):
    return pl.BlockSpec(shape, lambda *_: (0,) * len(shape), pipeline_mode=pl.Buffered(1))


def _ffn_body(x, g_ref, wg_ref, wu_ref, wd_ref, o_ref, act_ref):
    h = _rms(x, g_ref[...]).astype(BF16)
    for j in range(D_FF // FF_TILE):
        cols = slice(j * FF_TILE, (j + 1) * FF_TILE)
        gate = _dot(h, wg_ref[:, cols])
        up = _dot(h, wu_ref[:, cols])
        act_ref[:, cols] = (_silu(gate) * up).astype(BF16)
    y = _dot(act_ref[...], wd_ref[...])
    o_ref[...] = x + 0.5 * y


def _ffn_kernel(n_cast, x_ref, g_ref, wg_ref, wu_ref, wd_ref, *rest):
    cast_in, (o_ref, *cast_out, act_ref) = rest[:n_cast], rest[n_cast:]
    for src, dst in zip(cast_in, cast_out):
        dst[...] = src[...].astype(BF16)
    _ffn_body(x_ref[...], g_ref, wg_ref, wu_ref, wd_ref, o_ref, act_ref)


def _mix_ffn_kernel(x_ref, m_ref, wo_ref, g_ref, wg_ref, wu_ref, wd_ref, o_ref, act_ref):
    x = x_ref[...] + _dot(m_ref[...], wo_ref[...])
    _ffn_body(x, g_ref, wg_ref, wu_ref, wd_ref, o_ref, act_ref)


def _rows(width):
    return pl.BlockSpec((ROW_TILE, width), lambda i: (i, 0))


def _slab_spec(shape, steps):
    rows, width = shape
    per = 1 if (rows // steps) % BF16_ROWS == 0 and rows % steps == 0 else 2
    slab = rows * per // steps
    assert slab * steps == rows * per and slab % BF16_ROWS == 0
    return pl.BlockSpec((slab, width), lambda i: (i // per, 0))


def _ffn_call(x, g, wg, wu, wd, mix=None, cast=()):
    m = x.shape[0]
    steps = m // ROW_TILE
    w_specs = [_resident((1, D_MODEL)), _resident((D_MODEL, D_FF)),
               _resident((D_MODEL, D_FF)), _resident((D_FF, D_MODEL))]
    out_specs, out_shape = [_rows(D_MODEL)], [jax.ShapeDtypeStruct((m, D_MODEL), F32)]
    if mix is None:
        kern = functools.partial(_ffn_kernel, len(cast))
        ins = (x, g, wg, wu, wd) + tuple(cast)
        specs = [_rows(D_MODEL)] + w_specs + [_slab_spec(w.shape, steps) for w in cast]
        out_specs += [_slab_spec(w.shape, steps) for w in cast]
        out_shape += [jax.ShapeDtypeStruct(w.shape, BF16) for w in cast]
    else:
        assert not cast
        mixed, wo = mix
        kern = _mix_ffn_kernel
        ins = (x, mixed, wo, g, wg, wu, wd)
        specs = [_rows(D_MODEL), _rows(D_MODEL), _resident((D_MODEL, D_MODEL))] + w_specs
    return pl.pallas_call(
        kern,
        grid=(steps,),
        in_specs=specs,
        out_specs=out_specs,
        out_shape=out_shape,
        scratch_shapes=[pltpu.VMEM((ROW_TILE, D_FF), BF16)],
        compiler_params=pltpu.CompilerParams(
            dimension_semantics=("arbitrary",), vmem_limit_bytes=VMEM_LIMIT),
        name="mix_ffn" if mix is not None else "ffn",
    )(*ins)


def _projection_stages(x_ref, g_ref, w_ref, qg_ref, kg_ref, lbp_ref, seg_ref, h_ref, dst):
    names = ("q", "k", "v", "rq", "forget", "rv", "sg")
    per_group = ATTN_WIDTH // MXU_TILE

    def normalize():
        h_ref[...] = _rms(x_ref[0], g_ref[...]).astype(BF16)

    def project(i):
        return _dot(h_ref[...], w_ref[:, i * MXU_TILE:(i + 1) * MXU_TILE])

    def head_normed(a, gain):
        ms = _dot((a * a).astype(BF16), seg_ref[...]) * (1.0 / ATTN_HEAD_DIM)
        return (a * lax.rsqrt(ms + RMS_EPS) * gain).astype(BF16)

    def tail(i, val):
        group, part = divmod(i, per_group)
        cols = slice(part * MXU_TILE, (part + 1) * MXU_TILE)
        name = names[group]
        if name == "q":
            dst["q"](cols, head_normed(val, qg_ref[:, cols] * (ATTN_HEAD_DIM ** -0.5 * LOG2E)))
        elif name == "k":
            dst["k"](cols, head_normed(val, kg_ref[:, cols]))
        elif name in ("v", "rv"):
            dst[name](cols, val.astype(BF16))
        elif name in ("rq", "sg"):
            dst[name](cols, _silu(val).astype(BF16))
        else:
            lbp = lbp_ref[:, cols]
            e = jnp.exp(lbp - jnp.max(lbp, axis=0, keepdims=True))
            lb = e[0:1, :] / jnp.sum(e, axis=0, keepdims=True)
            en = jnp.exp(-jnp.abs(val))
            big, small = 1.0 / (1.0 + en), en / (1.0 + en)
            pos = val >= 0
            f = lb + (1.0 - lb) * jnp.where(pos, big, small)
            dst["lf"](cols, jnp.log2(f))
            dst["rk"](cols, ((1.0 - lb) * jnp.where(pos, small, big)).astype(BF16))

    return PROJ_GROUPS * per_group, normalize, (project, tail)


def _attention_stages(tile, src, bias_ref, o_ref):
    n_pairs = ATTN_HEADS // 2
    groups = MIX_ROWS // ATTN_TQ

    def where(u):
        g, p = divmod(u, n_pairs)
        group = tile * groups + g
        window = pl.ds(pl.multiple_of(group * ATTN_TQ, ATTN_TQ), ATTN_WIN)
        return group, window, slice(g * ATTN_TQ, (g + 1) * ATTN_TQ), slice(p * LANES, (p + 1) * LANES), p

    def scores(u):
        group, window, rows, cols, p = where(u)
        qt = src["qt"](cols, rows)
        zero = jnp.zeros((ATTN_HEAD_DIM, ATTN_TQ), BF16)
        q2 = jnp.concatenate([jnp.concatenate([qt[:ATTN_HEAD_DIM], zero], axis=0),
                              jnp.concatenate([zero, qt[ATTN_HEAD_DIM:]], axis=0)], axis=1)
        st = _dot(src["k"](window, cols), q2)
        first_valid = PAD // ATTN_TQ - group
        blocks = []
        for j in range(ATTN_WIN // ATTN_TQ):
            sj = st[j * ATTN_TQ:(j + 1) * ATTN_TQ]
            if j in BIAS_BLOCKS:
                sj = sj + bias_ref[p, BIAS_BLOCKS.index(j)]
            blocks.append(jnp.where(j >= first_valid, sj, -jnp.inf))
        return jnp.concatenate(blocks, axis=0)

    def weigh(u, st):
        _, window, rows, cols, _ = where(u)
        e = jnp.exp2(st - jnp.max(st, axis=0, keepdims=True)).astype(BF16)
        vt1 = jnp.concatenate([src["vt"](cols, window), src["ones"](window)], axis=0)
        ot = _dot(vt1, e)
        ot = ot[:LANES] / ot[LANES:LANES + 1]
        same_head = jnp.concatenate([ot[:ATTN_HEAD_DIM, :ATTN_TQ], ot[ATTN_HEAD_DIM:, ATTN_TQ:]], axis=0)
        o_ref[0, rows, cols] = same_head.T.astype(BF16)

    return groups * n_pairs, (scores, weigh)


def _level_sum_matrix():
    u = np.arange(CHUNK)[:, None]
    r = np.arange(CHUNK)[None, :]
    mats = [(r <= u)]
    for c in FINE_LEVELS:
        m = (u // (2 * c)) * (2 * c) + c
        upper = (u & c) != 0
        mats.append(np.where(upper, (r > m) & (r <= u), (r > u) & (r <= m)))
    return np.concatenate(mats, axis=0).astype(np.float32)


def _level_masks():
    t = np.arange(CHUNK)[:, None]
    s = np.arange(CHUNK)[None, :]
    return np.stack([t == s] + [(t ^ s) < 2 * c for c in LEVELS]).astype(np.float32)


def _hgrn_stages(src, lsum_ref, lmask_ref, og_ref, o_ref, state_ref):
    t_row = lax.broadcasted_iota(jnp.int32, (CHUNK, HGRN_HEAD_DIM), 0)
    esums = {}

    def where(n):
        u, hd = divmod(n, HGRN_HEADS)
        return u, slice(u * CHUNK, (u + 1) * CHUNK), slice(hd * HGRN_HEAD_DIM, (hd + 1) * HGRN_HEAD_DIM), hd

    def coarse_operands(q, k, b, c):
        qs, ks = [], []
        zero = jnp.zeros((c, HGRN_HEAD_DIM), F32)
        for blk in range(CHUNK // c):
            r = slice(blk * c, (blk + 1) * c)
            m = (blk // 2) * 2 * c + c
            b_m = b[m:m + 1, :]
            qs.append(q[r] * jnp.exp2(b[r] - b_m) if blk % 2 else zero)
            ks.append(zero if blk % 2 else k[r] * jnp.exp2(b_m - b[r]))
        return jnp.concatenate(qs, axis=0).astype(BF16), jnp.concatenate(ks, axis=0).astype(BF16)

    def fine_operands(q, k, w, c):
        upper = (t_row & c) != 0
        return (jnp.where(upper, q * w, 0.0).astype(BF16), jnp.where(upper, 0.0, k * w).astype(BF16))

    def products(n):
        u, rows, cols, _ = where(n)
        if u not in esums:
            esums[u] = _dot(lsum_ref[...], src["lf"](rows, slice(None)).astype(BF16))
        esum = esums[u]
        q = src["rq"](rows, cols).astype(F32)
        k = src["rk"](rows, cols).astype(F32)
        b = esum[:CHUNK, cols]
        b_last = b[CHUNK - 1:CHUNK, :]
        prods = []
        for c in LEVELS:
            if c in FINE_LEVELS:
                i = 1 + FINE_LEVELS.index(c)
                qw, kw = fine_operands(q, k, jnp.exp2(esum[i * CHUNK:(i + 1) * CHUNK, cols]), c)
            else:
                qw, kw = coarse_operands(q, k, b, c)
            prods.append(_dot_nt(qw, kw))
        return dict(q_in=(q * jnp.exp2(b)).astype(BF16), k_up=(k * jnp.exp2(b_last - b)).astype(BF16),
                    decay=jnp.exp2(b_last), diag=jnp.sum(q * k, axis=-1, keepdims=True), prods=prods)

    def finish(n, a):
        _, rows, cols, hd = where(n)
        scores = a["diag"] * lmask_ref[0]
        for lv in range(N_LEVELS):
            scores = scores + a["prods"][lv] * lmask_ref[1 + lv]
        v = src["rv"](rows, cols)
        state = state_ref[hd]
        o = _dot(a["q_in"], state.astype(BF16)) + _dot(scores.astype(BF16), v)
        decay = jnp.broadcast_to(a["decay"], (HGRN_HEAD_DIM, HGRN_HEAD_DIM)).T
        state_ref[hd] = state * decay + _dot_tn(a["k_up"], v)
        y = _rms(o, og_ref[...]) * src["sg"](rows, cols).astype(F32)
        out_cols = slice(ATTN_WIDTH + hd * HGRN_HEAD_DIM, ATTN_WIDTH + (hd + 1) * HGRN_HEAD_DIM)
        o_ref[0, rows, out_cols] = y.astype(BF16)

    return (MIX_ROWS // CHUNK) * HGRN_HEADS, (products, finish)


def _mixer_kernel(x0_ref, xn_ref, g_ref, w_ref, qg_ref, kg_ref, lbp_ref, seg_ref, bias_ref, lsum_ref,
                  lmask_ref, og_ref, o_ref, h_ref, qt_ref, rq_ref, rk_ref, rv_ref, sg_ref, lf_ref,
                  kpad_ref, vt_ref, ones_ref, state_ref):
    seq = kpad_ref.shape[1] - PAD
    tiles = seq // MIX_ROWS
    s = pl.program_id(0)
    tile = lax.rem(s, tiles)
    nxt = jnp.minimum(s + 1, pl.num_programs(0) - 1)

    def projection(x_ref, step):
        slot = lax.rem(step, 2)
        par = lax.rem(lax.div(step, tiles), 2)
        rows = pl.ds(pl.multiple_of(PAD + lax.rem(step, tiles) * MIX_ROWS, MIX_ROWS), MIX_ROWS)

        def put(ref):
            def store(cols, val):
                ref[slot, :, cols] = val
            return store

        def put_k(cols, val):
            kpad_ref[par, rows, cols] = val

        def put_v(cols, val):
            vt_ref[par, cols, rows] = val.T

        def put_q(cols, val):
            qt_ref[slot, cols, :] = val.T

        dst = dict(q=put_q, k=put_k, v=put_v, rq=put(rq_ref), rk=put(rk_ref), rv=put(rv_ref),
                   sg=put(sg_ref), lf=put(lf_ref))
        return _projection_stages(x_ref, g_ref, w_ref, qg_ref, kg_ref, lbp_ref, seg_ref, h_ref, dst)

    @pl.when(s == 0)
    def _():
        kpad_ref[:, :PAD, :] = jnp.zeros((2, PAD, ATTN_WIDTH), BF16)
        vt_ref[:, :, :PAD] = jnp.zeros((2, ATTN_WIDTH, PAD), BF16)
        ones_ref[:, :PAD] = jnp.zeros((BF16_ROWS, PAD), BF16)
        ones_ref[:, PAD:] = jnp.ones((BF16_ROWS, seq), BF16)
        n_units, normalize, (project, tail) = projection(x0_ref, s)
        normalize()
        nxt_val = project(0)
        for i in range(n_units):
            cur, nxt_val = nxt_val, (project(i + 1) if i + 1 < n_units else None)
            tail(i, cur)

    @pl.when(tile == 0)
    def _():
        state_ref[...] = jnp.zeros_like(state_ref)

    slot = lax.rem(s, 2)
    par = lax.rem(lax.div(s, tiles), 2)
    tile_src = lambda ref: (lambda rows, cols: ref[slot, rows, cols])
    src = dict(qt=tile_src(qt_ref), rq=tile_src(rq_ref), rk=tile_src(rk_ref), rv=tile_src(rv_ref),
               sg=tile_src(sg_ref), lf=tile_src(lf_ref),
               k=lambda window, cols: kpad_ref[par, window, cols],
               vt=lambda cols, window: vt_ref[par, cols, window],
               ones=lambda window: ones_ref[:, window])
    n_proj, normalize, (proj_first, proj_second) = projection(xn_ref, nxt)
    n_attn, (attn_first, attn_second) = _attention_stages(tile, src, bias_ref, o_ref)
    n_hgrn, (hgrn_first, hgrn_second) = _hgrn_stages(src, lsum_ref, lmask_ref, og_ref, o_ref, state_ref)

    per = n_hgrn // n_attn
    normalize()
    p_next, a_next, h_next = proj_first(0), attn_first(0), hgrn_first(0)
    for i in range(n_attn):
        a_cur, a_next = a_next, (attn_first(i + 1) if i + 1 < n_attn else None)
        for n in range(i * per, (i + 1) * per):
            h_cur, h_next = h_next, (hgrn_first(n + 1) if n + 1 < n_hgrn else None)
            hgrn_second(n, h_cur)
            if n == i * per:
                attn_second(i, a_cur)
                if i < n_proj:
                    p_cur, p_next = p_next, (proj_first(i + 1) if i + 1 < n_proj else None)
                    proj_second(i, p_cur)


def _mixer_call(x1, g, w_in, qg, kg, lbp, seg, bias, lsum, lmask, og):
    b, s, _ = x1.shape
    tiles = s // MIX_ROWS
    steps = b * tiles
    slots = lambda dt: pltpu.VMEM((2, MIX_ROWS, ATTN_WIDTH), dt)

    def next_tile(i):
        n = jnp.minimum(i + 1, steps - 1)
        return (n // tiles, n % tiles, 0)

    return pl.pallas_call(
        _mixer_kernel,
        grid=(steps,),
        in_specs=[pl.BlockSpec((1, MIX_ROWS, D_MODEL), lambda i: (0, 0, 0), pipeline_mode=pl.Buffered(1)),
                  pl.BlockSpec((1, MIX_ROWS, D_MODEL), next_tile),
                  _resident((1, D_MODEL)), _resident((D_MODEL, PROJ_COLS)),
                  _resident((1, ATTN_WIDTH)), _resident((1, ATTN_WIDTH)), _resident(lbp.shape),
                  _resident((MXU_TILE, MXU_TILE)), _resident(bias.shape), _resident(lsum.shape),
                  _resident(lmask.shape), _resident((1, HGRN_HEAD_DIM))],
        out_specs=pl.BlockSpec((1, MIX_ROWS, D_MODEL), lambda i: (i // tiles, i % tiles, 0)),
        out_shape=jax.ShapeDtypeStruct((b, s, D_MODEL), BF16),
        scratch_shapes=[pltpu.VMEM((MIX_ROWS, D_MODEL), BF16),
                        pltpu.VMEM((2, ATTN_WIDTH, MIX_ROWS), BF16), slots(BF16), slots(BF16), slots(BF16), slots(BF16), slots(F32),
                        pltpu.VMEM((2, s + PAD, ATTN_WIDTH), BF16),
                        pltpu.VMEM((2, ATTN_WIDTH, s + PAD), BF16),
                        pltpu.VMEM((BF16_ROWS, s + PAD), BF16),
                        pltpu.VMEM((HGRN_HEADS, HGRN_HEAD_DIM, HGRN_HEAD_DIM), F32)],
        compiler_params=pltpu.CompilerParams(
            dimension_semantics=("arbitrary",), vmem_limit_bytes=VMEM_LIMIT),
        name="mixer",
    )(x1, x1, g, w_in, qg, kg, lbp, seg, bias, lsum, lmask, og)


def _bias_blocks():
    qq = np.arange(ATTN_TQ)[:, None]
    key = np.arange(ATTN_WIN)[None, :]
    near = PAD + qq - key < REL_CLIP
    band = (key // CHUNK >= qq // CHUNK) & (key // CHUNK <= qq // CHUNK + LEFT_CHUNKS)
    used = (near | ~band).any(axis=0).reshape(ATTN_WIN // ATTN_TQ, ATTN_TQ).any(axis=1)
    return tuple(int(j) for j in np.nonzero(used)[0])


BIAS_BLOCKS = _bias_blocks()


def _rel_bias_table(rel_bias):
    assert ATTN_TQ - 1 <= REL_CLIP
    rb = (rel_bias.astype(F32) - rel_bias.astype(F32)[:, 2 * REL_CLIP:]) * LOG2E
    n = ATTN_WIN + ATTN_TQ
    far = jnp.broadcast_to(rb[:, 2 * REL_CLIP:], (ATTN_HEADS, ATTN_WIN - REL_CLIP))
    near = rb[:, REL_CLIP - ATTN_TQ + 1:2 * REL_CLIP][:, ::-1]
    x = jnp.concatenate([far, near, jnp.zeros((ATTN_HEADS, 1), F32)], axis=1)
    rolled = jnp.tile(x, (1, ATTN_TQ))[:, :ATTN_TQ * (n - 1)].reshape(ATTN_HEADS, ATTN_TQ, n - 1)
    tbl = rolled[:, :, ATTN_TQ - 1:ATTN_TQ - 1 + ATTN_WIN]
    qc = np.arange(ATTN_TQ)[:, None] // CHUNK
    kc = np.arange(ATTN_WIN)[None, :] // CHUNK
    tbl = jnp.where((kc >= qc) & (kc <= qc + LEFT_CHUNKS), tbl, -jnp.inf)
    tbl = tbl.reshape(ATTN_HEADS // 2, 2, ATTN_TQ, ATTN_WIN).transpose(0, 3, 1, 2)
    tbl = tbl.reshape(ATTN_HEADS // 2, ATTN_WIN // ATTN_TQ, ATTN_TQ, 2 * ATTN_TQ)
    return tbl[:, np.asarray(BIAS_BLOCKS)]


def kernel(x, ffn1_norm_g, ffn1_w_gate, ffn1_w_up, ffn1_w_down, mix_norm_g, w_in,
           attn_q_norm_g, attn_k_norm_g, attn_rel_bias, hgrn_lower_bounds, hgrn_out_norm_g,
           w_out, ffn2_norm_g, ffn2_w_gate, ffn2_w_up, ffn2_w_down):
    bsz, seq, _ = x.shape
    depth = ffn1_norm_g.shape[0]
    assert depth == 1 and seq % MIX_ROWS == 0 and (bsz * seq) % ROW_TILE == 0
    head_of_col = np.arange(MXU_TILE) // ATTN_HEAD_DIM
    seg = jnp.asarray(head_of_col[:, None] == head_of_col[None, :], dtype=BF16)
    lsum = jnp.asarray(_level_sum_matrix(), dtype=BF16)
    lmask = jnp.asarray(_level_masks())
    row = lambda g: g.reshape(1, -1).astype(F32)
    tile_heads = lambda g: jnp.tile(g.astype(F32), ATTN_HEADS).reshape(1, ATTN_WIDTH)

    xf = x.reshape(bsz * seq, D_MODEL)
    for l in range(depth):
        later = (w_in[l], w_out[l], ffn2_w_gate[l], ffn2_w_up[l], ffn2_w_down[l])
        x1, w_in_b, w_out_b, wg2_b, wu2_b, wd2_b = _ffn_call(
            xf, row(ffn1_norm_g[l]), ffn1_w_gate[l].astype(BF16), ffn1_w_up[l].astype(BF16),
            ffn1_w_down[l].astype(BF16), cast=tuple(w.astype(F32) for w in later))
        mixed = _mixer_call(
            x1.reshape(bsz, seq, D_MODEL), row(mix_norm_g[l]), w_in_b,
            tile_heads(attn_q_norm_g[l]), tile_heads(attn_k_norm_g[l]), hgrn_lower_bounds.astype(F32),
            seg, _rel_bias_table(attn_rel_bias[l]), lsum, lmask, row(hgrn_out_norm_g[l]))
        xf, = _ffn_call(x1, row(ffn2_norm_g[l]), wg2_b, wu2_b, wd2_b,
                        mix=(mixed.reshape(bsz * seq, D_MODEL), w_out_b))
    return xf.reshape(bsz, seq, D_MODEL)
```

```python
import functools

import numpy as np
import jax
import jax.numpy as jnp
from jax import lax
from jax.experimental import pallas as pl
from jax.experimental.pallas import tpu as pltpu

D_MODEL = 1024
CHUNK = 64
ATTN_WIDTH = 512
HGRN_WIDTH = 512
ATTN_HEAD_DIM = 64
ATTN_HEADS = 8
HGRN_HEAD_DIM = 128
HGRN_HEADS = 4
LEFT_CHUNKS = 8
PAD = LEFT_CHUNKS * CHUNK
REL_CLIP = 128
D_FF = 2816
RMS_EPS = 1e-6
PROJ_GROUPS = 7
PROJ_COLS = PROJ_GROUPS * ATTN_WIDTH

LANES = 128
SUBLANES = 8
BF16_ROWS = 2 * SUBLANES
MXU_TILE = 256
FF_TILE = MXU_TILE
ROW_TILE = 1024
LOG2E = float(np.log2(np.e))
ATTN_Q_CHUNKS = 2
ATTN_TQ = ATTN_Q_CHUNKS * CHUNK
ATTN_WIN = PAD + ATTN_TQ
MIX_ROWS = 512
N_LEVELS = 6
LEVELS = tuple(CHUNK >> (lv + 1) for lv in range(N_LEVELS))
FINE_LEVELS = tuple(c for c in LEVELS if c < SUBLANES)
VMEM_LIMIT = 56 * 1024 * 1024

BF16 = jnp.bfloat16
F32 = jnp.float32


def _dot(a, b):
    return jnp.dot(a, b, preferred_element_type=F32)


def _dot_nt(a, b):
    return lax.dot_general(a, b, (((1,), (1,)), ((), ())), preferred_element_type=F32)


def _dot_tn(a, b):
    return lax.dot_general(a, b, (((0,), (0,)), ((), ())), preferred_element_type=F32)


def _silu(x):
    return x / (1.0 + jnp.exp(-x))


def _rms(x, g):
    ms = jnp.mean(x * x, axis=-1, keepdims=True)
    return x * lax.rsqrt(ms + RMS_EPS) * g


def _resident(shape):
    return pl.BlockSpec(shape, lambda *_: (0,) * len(shape), pipeline_mode=pl.Buffered(1))


def _ffn_body(x, g_ref, wg_ref, wu_ref, wd_ref, o_ref, act_ref):
    h = _rms(x, g_ref[...]).astype(BF16)
    for j in range(D_FF // FF_TILE):
        cols = slice(j * FF_TILE, (j + 1) * FF_TILE)
        gate = _dot(h, wg_ref[:, cols])
        up = _dot(h, wu_ref[:, cols])
        act_ref[:, cols] = (_silu(gate) * up).astype(BF16)
    y = _dot(act_ref[...], wd_ref[...])
    o_ref[...] = x + 0.5 * y


def _ffn_kernel(n_cast, x_ref, g_ref, wg_ref, wu_ref, wd_ref, *rest):
    cast_in, (o_ref, *cast_out, act_ref) = rest[:n_cast], rest[n_cast:]
    for src, dst in zip(cast_in, cast_out):
        dst[...] = src[...].astype(BF16)
    _ffn_body(x_ref[...], g_ref, wg_ref, wu_ref, wd_ref, o_ref, act_ref)


def _mix_ffn_kernel(x_ref, m_ref, wo_ref, g_ref, wg_ref, wu_ref, wd_ref, o_ref, act_ref):
    x = x_ref[...] + _dot(m_ref[...], wo_ref[...])
    _ffn_body(x, g_ref, wg_ref, wu_ref, wd_ref, o_ref, act_ref)


def _rows(width):
    return pl.BlockSpec((ROW_TILE, width), lambda i: (i, 0))


def _slab_spec(shape, steps):
    rows, width = shape
    per = 1 if (rows // steps) % BF16_ROWS == 0 and rows % steps == 0 else 2
    slab = rows * per // steps
    assert slab * steps == rows * per and slab % BF16_ROWS == 0
    return pl.BlockSpec((slab, width), lambda i: (i // per, 0))


def _ffn_call(x, g, wg, wu, wd, mix=None, cast=()):
    m = x.shape[0]
    steps = m // ROW_TILE
    w_specs = [_resident((1, D_MODEL)), _resident((D_MODEL, D_FF)),
               _resident((D_MODEL, D_FF)), _resident((D_FF, D_MODEL))]
    out_specs, out_shape = [_rows(D_MODEL)], [jax.ShapeDtypeStruct((m, D_MODEL), F32)]
    if mix is None:
        kern = functools.partial(_ffn_kernel, len(cast))
        ins = (x, g, wg, wu, wd) + tuple(cast)
        specs = [_rows(D_MODEL)] + w_specs + [_slab_spec(w.shape, steps) for w in cast]
        out_specs += [_slab_spec(w.shape, steps) for w in cast]
        out_shape += [jax.ShapeDtypeStruct(w.shape, BF16) for w in cast]
    else:
        assert not cast
        mixed, wo = mix
        kern = _mix_ffn_kernel
        ins = (x, mixed, wo, g, wg, wu, wd)
        specs = [_rows(D_MODEL), _rows(D_MODEL), _resident((D_MODEL, D_MODEL))] + w_specs
    return pl.pallas_call(
        kern,
        grid=(steps,),
        in_specs=specs,
        out_specs=out_specs,
        out_shape=out_shape,
        scratch_shapes=[pltpu.VMEM((ROW_TILE, D_FF), BF16)],
        compiler_params=pltpu.CompilerParams(
            dimension_semantics=("arbitrary",), vmem_limit_bytes=VMEM_LIMIT),
        name="mix_ffn" if mix is not None else "ffn",
    )(*ins)


def _projection_stages(x_ref, g_ref, w_ref, qg_ref, kg_ref, lbp_ref, seg_ref, h_ref, dst):
    names = ("q", "k", "v", "rq", "forget", "rv", "sg")
    per_group = ATTN_WIDTH // MXU_TILE

    def normalize():
        h_ref[...] = _rms(x_ref[0], g_ref[...]).astype(BF16)

    def project(i):
        return _dot(h_ref[...], w_ref[:, i * MXU_TILE:(i + 1) * MXU_TILE])

    def head_normed(a, gain):
        ms = _dot((a * a).astype(BF16), seg_ref[...]) * (1.0 / ATTN_HEAD_DIM)
        return (a * lax.rsqrt(ms + RMS_EPS) * gain).astype(BF16)

    def tail(i, val):
        group, part = divmod(i, per_group)
        cols = slice(part * MXU_TILE, (part + 1) * MXU_TILE)
        name = names[group]
        if name == "q":
            dst["q"](cols, head_normed(val, qg_ref[:, cols] * (ATTN_HEAD_DIM ** -0.5 * LOG2E)))
        elif name == "k":
            dst["k"](cols, head_normed(val, kg_ref[:, cols]))
        elif name in ("v", "rv"):
            dst[name](cols, val.astype(BF16))
        elif name in ("rq", "sg"):
            dst[name](cols, _silu(val).astype(BF16))
        else:
            lbp = lbp_ref[:, cols]
            e = jnp.exp(lbp - jnp.max(lbp, axis=0, keepdims=True))
            lb = e[0:1, :] / jnp.sum(e, axis=0, keepdims=True)
            en = jnp.exp(-jnp.abs(val))
            big, small = 1.0 / (1.0 + en), en / (1.0 + en)
            pos = val >= 0
            f = lb + (1.0 - lb) * jnp.where(pos, big, small)
            dst["lf"](cols, jnp.log2(f).astype(BF16))
            dst["rk"](cols, ((1.0 - lb) * jnp.where(pos, small, big)).astype(BF16))

    return PROJ_GROUPS * per_group, normalize, (project, tail)


def _attention_stages(tile, src, bias_ref, o_ref):
    n_pairs = ATTN_HEADS // 2
    groups = MIX_ROWS // ATTN_TQ

    def where(u):
        g, p = divmod(u, n_pairs)
        group = tile * groups + g
        window = pl.ds(pl.multiple_of(group * ATTN_TQ, ATTN_TQ), ATTN_WIN)
        return group, window, slice(g * ATTN_TQ, (g + 1) * ATTN_TQ), slice(p * LANES, (p + 1) * LANES), p

    def scores(u):
        group, window, rows, cols, p = where(u)
        qt = src["qt"](cols, rows)
        zero = jnp.zeros((ATTN_HEAD_DIM, ATTN_TQ), BF16)
        q2 = jnp.concatenate([jnp.concatenate([qt[:ATTN_HEAD_DIM], zero], axis=0),
                              jnp.concatenate([zero, qt[ATTN_HEAD_DIM:]], axis=0)], axis=1)
        st = _dot(src["k"](window, cols), q2)
        first_valid = PAD // ATTN_TQ - group
        blocks = []
        for j in range(ATTN_WIN // ATTN_TQ):
            sj = st[j * ATTN_TQ:(j + 1) * ATTN_TQ]
            if j in BIAS_BLOCKS:
                sj = sj + bias_ref[p, BIAS_BLOCKS.index(j)]
            blocks.append(jnp.where(j >= first_valid, sj, -jnp.inf))
        return jnp.concatenate(blocks, axis=0)

    def weigh(u, st):
        _, window, rows, cols, _ = where(u)
        e = jnp.exp2(st - jnp.max(st, axis=0, keepdims=True)).astype(BF16)
        vt1 = jnp.concatenate([src["vt"](cols, window), src["ones"](window)], axis=0)
        ot = _dot(vt1, e)
        ot = ot[:LANES] / ot[LANES:LANES + 1]
        same_head = jnp.concatenate([ot[:ATTN_HEAD_DIM, :ATTN_TQ], ot[ATTN_HEAD_DIM:, ATTN_TQ:]], axis=0)
        o_ref[0, rows, cols] = same_head.T.astype(BF16)

    return groups * n_pairs, (scores, weigh)


def _level_sum_matrix():
    u = np.arange(CHUNK)[:, None]
    r = np.arange(CHUNK)[None, :]
    mats = [(r <= u)]
    for c in FINE_LEVELS:
        m = (u // (2 * c)) * (2 * c) + c
        upper = (u & c) != 0
        mats.append(np.where(upper, (r > m) & (r <= u), (r > u) & (r <= m)))
    return np.concatenate(mats, axis=0).astype(np.float32)


def _level_masks():
    t = np.arange(CHUNK)[:, None]
    s = np.arange(CHUNK)[None, :]
    return np.stack([t == s] + [(t ^ s) < 2 * c for c in LEVELS]).astype(np.float32)


def _hgrn_stages(src, lsum_ref, lmask_ref, og_ref, o_ref, state_ref):
    t_row = lax.broadcasted_iota(jnp.int32, (CHUNK, HGRN_HEAD_DIM), 0)
    esums = {}

    def where(n):
        u, hd = divmod(n, HGRN_HEADS)
        return u, slice(u * CHUNK, (u + 1) * CHUNK), slice(hd * HGRN_HEAD_DIM, (hd + 1) * HGRN_HEAD_DIM), hd

    def coarse_operands(q, k, b, c):
        qs, ks = [], []
        zero = jnp.zeros((c, HGRN_HEAD_DIM), F32)
        for blk in range(CHUNK // c):
            r = slice(blk * c, (blk + 1) * c)
            m = (blk // 2) * 2 * c + c
            b_m = b[m:m + 1, :]
            qs.append(q[r] * jnp.exp2(b[r] - b_m) if blk % 2 else zero)
            ks.append(zero if blk % 2 else k[r] * jnp.exp2(b_m - b[r]))
        return jnp.concatenate(qs, axis=0).astype(BF16), jnp.concatenate(ks, axis=0).astype(BF16)

    def fine_operands(q, k, w, c):
        upper = (t_row & c) != 0
        return (jnp.where(upper, q * w, 0.0).astype(BF16), jnp.where(upper, 0.0, k * w).astype(BF16))

    def products(n):
        u, rows, cols, _ = where(n)
        if u not in esums:
            esums[u] = _dot(lsum_ref[...], src["lf"](rows, slice(None)))
        esum = esums[u]
        q = src["rq"](rows, cols).astype(F32)
        k = src["rk"](rows, cols).astype(F32)
        b = esum[:CHUNK, cols]
        b_last = b[CHUNK - 1:CHUNK, :]
        prods = []
        for c in LEVELS:
            if c in FINE_LEVELS:
                i = 1 + FINE_LEVELS.index(c)
                qw, kw = fine_operands(q, k, jnp.exp2(esum[i * CHUNK:(i + 1) * CHUNK, cols]), c)
            else:
                qw, kw = coarse_operands(q, k, b, c)
            prods.append(_dot_nt(qw, kw))
        return dict(q_in=(q * jnp.exp2(b)).astype(BF16), k_up=(k * jnp.exp2(b_last - b)).astype(BF16),
                    decay=jnp.exp2(b_last), diag=jnp.sum(q * k, axis=-1, keepdims=True), prods=prods)

    def finish(n, a):
        _, rows, cols, hd = where(n)
        scores = a["diag"] * lmask_ref[0]
        for lv in range(N_LEVELS):
            scores = scores + a["prods"][lv] * lmask_ref[1 + lv]
        v = src["rv"](rows, cols)
        state = state_ref[hd]
        o = _dot_nt(a["q_in"], state.astype(BF16)) + _dot(scores.astype(BF16), v)
        state_ref[hd] = state * a["decay"] + _dot_tn(v, a["k_up"])
        y = _rms(o, og_ref[...]) * src["sg"](rows, cols).astype(F32)
        out_cols = slice(ATTN_WIDTH + hd * HGRN_HEAD_DIM, ATTN_WIDTH + (hd + 1) * HGRN_HEAD_DIM)
        o_ref[0, rows, out_cols] = y.astype(BF16)

    return (MIX_ROWS // CHUNK) * HGRN_HEADS, (products, finish)


def _mixer_kernel(x0_ref, xn_ref, g_ref, w_ref, qg_ref, kg_ref, lbp_ref, seg_ref, bias_ref, lsum_ref,
                  lmask_ref, og_ref, o_ref, h_ref, qt_ref, rq_ref, rk_ref, rv_ref, sg_ref, lf_ref,
                  kpad_ref, vt_ref, ones_ref, state_ref):
    seq = kpad_ref.shape[1] - PAD
    tiles = seq // MIX_ROWS
    s = pl.program_id(0)
    tile = lax.rem(s, tiles)
    nxt = jnp.minimum(s + 1, pl.num_programs(0) - 1)

    def projection(x_ref, step):
        slot = lax.rem(step, 2)
        par = lax.rem(lax.div(step, tiles), 2)
        rows = pl.ds(pl.multiple_of(PAD + lax.rem(step, tiles) * MIX_ROWS, MIX_ROWS), MIX_ROWS)

        def put(ref):
            def store(cols, val):
                ref[slot, :, cols] = val
            return store

        def put_k(cols, val):
            kpad_ref[par, rows, cols] = val

        def put_v(cols, val):
            vt_ref[par, cols, rows] = val.T

        def put_q(cols, val):
            qt_ref[slot, cols, :] = val.T

        dst = dict(q=put_q, k=put_k, v=put_v, rq=put(rq_ref), rk=put(rk_ref), rv=put(rv_ref),
                   sg=put(sg_ref), lf=put(lf_ref))
        return _projection_stages(x_ref, g_ref, w_ref, qg_ref, kg_ref, lbp_ref, seg_ref, h_ref, dst)

    @pl.when(s == 0)
    def _():
        kpad_ref[:, :PAD, :] = jnp.zeros((2, PAD, ATTN_WIDTH), BF16)
        vt_ref[:, :, :PAD] = jnp.zeros((2, ATTN_WIDTH, PAD), BF16)
        ones_ref[:, :PAD] = jnp.zeros((BF16_ROWS, PAD), BF16)
        ones_ref[:, PAD:] = jnp.ones((BF16_ROWS, seq), BF16)
        n_units, normalize, (project, tail) = projection(x0_ref, s)
        normalize()
        nxt_val = project(0)
        for i in range(n_units):
            cur, nxt_val = nxt_val, (project(i + 1) if i + 1 < n_units else None)
            tail(i, cur)

    @pl.when(tile == 0)
    def _():
        state_ref[...] = jnp.zeros_like(state_ref)

    slot = lax.rem(s, 2)
    par = lax.rem(lax.div(s, tiles), 2)
    tile_src = lambda ref: (lambda rows, cols: ref[slot, rows, cols])
    src = dict(qt=tile_src(qt_ref), rq=tile_src(rq_ref), rk=tile_src(rk_ref), rv=tile_src(rv_ref),
               sg=tile_src(sg_ref), lf=tile_src(lf_ref),
               k=lambda window, cols: kpad_ref[par, window, cols],
               vt=lambda cols, window: vt_ref[par, cols, window],
               ones=lambda window: ones_ref[:, window])
    n_proj, normalize, (proj_first, proj_second) = projection(xn_ref, nxt)
    n_attn, (attn_first, attn_second) = _attention_stages(tile, src, bias_ref, o_ref)
    n_hgrn, (hgrn_first, hgrn_second) = _hgrn_stages(src, lsum_ref, lmask_ref, og_ref, o_ref, state_ref)

    assert n_proj <= n_attn and n_hgrn % n_attn == 0
    per = n_hgrn // n_attn
    normalize()
    p_next, a_next, h_next = proj_first(0), attn_first(0), hgrn_first(0)
    for i in range(n_attn):
        a_cur, a_next = a_next, (attn_first(i + 1) if i + 1 < n_attn else None)
        for n in range(i * per, (i + 1) * per):
            h_cur, h_next = h_next, (hgrn_first(n + 1) if n + 1 < n_hgrn else None)
            hgrn_second(n, h_cur)
            if n == i * per:
                attn_second(i, a_cur)
                if i < n_proj:
                    p_cur, p_next = p_next, (proj_first(i + 1) if i + 1 < n_proj else None)
                    proj_second(i, p_cur)


def _mixer_call(x1, g, w_in, qg, kg, lbp, seg, bias, lsum, lmask, og):
    b, s, _ = x1.shape
    tiles = s // MIX_ROWS
    steps = b * tiles
    slot_pair = pltpu.VMEM((2, MIX_ROWS, HGRN_WIDTH), BF16)

    def next_tile(i):
        n = jnp.minimum(i + 1, steps - 1)
        return (n // tiles, n % tiles, 0)

    return pl.pallas_call(
        _mixer_kernel,
        grid=(steps,),
        in_specs=[pl.BlockSpec((1, MIX_ROWS, D_MODEL), lambda i: (0, 0, 0), pipeline_mode=pl.Buffered(1)),
                  pl.BlockSpec((1, MIX_ROWS, D_MODEL), next_tile),
                  _resident((1, D_MODEL)), _resident((D_MODEL, PROJ_COLS)),
                  _resident((1, ATTN_WIDTH)), _resident((1, ATTN_WIDTH)), _resident(lbp.shape),
                  _resident((MXU_TILE, MXU_TILE)), _resident(bias.shape), _resident(lsum.shape),
                  _resident(lmask.shape), _resident((1, HGRN_HEAD_DIM))],
        out_specs=pl.BlockSpec((1, MIX_ROWS, D_MODEL), lambda i: (i // tiles, i % tiles, 0)),
        out_shape=jax.ShapeDtypeStruct((b, s, D_MODEL), BF16),
        scratch_shapes=[pltpu.VMEM((MIX_ROWS, D_MODEL), BF16),
                        pltpu.VMEM((2, ATTN_WIDTH, MIX_ROWS), BF16)] + [slot_pair] * 5 + [
                        pltpu.VMEM((2, s + PAD, ATTN_WIDTH), BF16),
                        pltpu.VMEM((2, ATTN_WIDTH, s + PAD), BF16),
                        pltpu.VMEM((BF16_ROWS, s + PAD), BF16),
                        pltpu.VMEM((HGRN_HEADS, HGRN_HEAD_DIM, HGRN_HEAD_DIM), F32)],
        compiler_params=pltpu.CompilerParams(
            dimension_semantics=("arbitrary",), vmem_limit_bytes=VMEM_LIMIT),
        name="mixer",
    )(x1, x1, g, w_in, qg, kg, lbp, seg, bias, lsum, lmask, og)


def _bias_blocks():
    qq = np.arange(ATTN_TQ)[:, None]
    key = np.arange(ATTN_WIN)[None, :]
    near = PAD + qq - key < REL_CLIP
    band = (key // CHUNK >= qq // CHUNK) & (key // CHUNK <= qq // CHUNK + LEFT_CHUNKS)
    used = (near | ~band).any(axis=0).reshape(ATTN_WIN // ATTN_TQ, ATTN_TQ).any(axis=1)
    return tuple(int(j) for j in np.nonzero(used)[0])


BIAS_BLOCKS = _bias_blocks()


def _rel_bias_table(rel_bias):
    assert ATTN_TQ - 1 <= REL_CLIP
    rb = (rel_bias.astype(F32) - rel_bias.astype(F32)[:, 2 * REL_CLIP:]) * LOG2E
    n = ATTN_WIN + ATTN_TQ
    far = jnp.broadcast_to(rb[:, 2 * REL_CLIP:], (ATTN_HEADS, ATTN_WIN - REL_CLIP))
    near = rb[:, REL_CLIP - ATTN_TQ + 1:2 * REL_CLIP][:, ::-1]
    x = jnp.concatenate([far, near, jnp.zeros((ATTN_HEADS, 1), F32)], axis=1)
    rolled = jnp.tile(x, (1, ATTN_TQ))[:, :ATTN_TQ * (n - 1)].reshape(ATTN_HEADS, ATTN_TQ, n - 1)
    tbl = rolled[:, :, ATTN_TQ - 1:ATTN_TQ - 1 + ATTN_WIN]
    qc = np.arange(ATTN_TQ)[:, None] // CHUNK
    kc = np.arange(ATTN_WIN)[None, :] // CHUNK
    tbl = jnp.where((kc >= qc) & (kc <= qc + LEFT_CHUNKS), tbl, -jnp.inf)
    tbl = tbl.reshape(ATTN_HEADS // 2, 2, ATTN_TQ, ATTN_WIN).transpose(0, 3, 1, 2)
    tbl = tbl.reshape(ATTN_HEADS // 2, ATTN_WIN // ATTN_TQ, ATTN_TQ, 2 * ATTN_TQ)
    return tbl[:, np.asarray(BIAS_BLOCKS)]


def kernel(x, ffn1_norm_g, ffn1_w_gate, ffn1_w_up, ffn1_w_down, mix_norm_g, w_in,
           attn_q_norm_g, attn_k_norm_g, attn_rel_bias, hgrn_lower_bounds, hgrn_out_norm_g,
           w_out, ffn2_norm_g, ffn2_w_gate, ffn2_w_up, ffn2_w_down):
    bsz, seq, _ = x.shape
    depth = ffn1_norm_g.shape[0]
    assert depth == 1 and seq % MIX_ROWS == 0 and (bsz * seq) % ROW_TILE == 0
    head_of_col = np.arange(MXU_TILE) // ATTN_HEAD_DIM
    seg = jnp.asarray(head_of_col[:, None] == head_of_col[None, :], dtype=BF16)
    lsum = jnp.asarray(_level_sum_matrix(), dtype=BF16)
    lmask = jnp.asarray(_level_masks())
    row = lambda g: g.reshape(1, -1).astype(F32)
    tile_heads = lambda g: jnp.tile(g.astype(F32), ATTN_HEADS).reshape(1, ATTN_WIDTH)

    xf = x.reshape(bsz * seq, D_MODEL)
    for l in range(depth):
        later = (w_in[l], w_out[l], ffn2_w_gate[l], ffn2_w_up[l], ffn2_w_down[l])
        x1, w_in_b, w_out_b, wg2_b, wu2_b, wd2_b = _ffn_call(
            xf, row(ffn1_norm_g[l]), ffn1_w_gate[l].astype(BF16), ffn1_w_up[l].astype(BF16),
            ffn1_w_down[l].astype(BF16), cast=tuple(w.astype(F32) for w in later))
        mixed = _mixer_call(
            x1.reshape(bsz, seq, D_MODEL), row(mix_norm_g[l]), w_in_b,
            tile_heads(attn_q_norm_g[l]), tile_heads(attn_k_norm_g[l]), hgrn_lower_bounds.astype(F32),
            seg, _rel_bias_table(attn_rel_bias[l]), lsum, lmask, row(hgrn_out_norm_g[l]))
        xf, = _ffn_call(x1, row(ffn2_norm_g[l]), wg2_b, wu2_b, wd2_b,
                        mix=(mixed.reshape(bsz * seq, D_MODEL), w_out_b))
    return xf.reshape(bsz, seq, D_MODEL)
```

```python
import functools

import numpy as np
import jax
import jax.numpy as jnp
from jax import lax
from jax.experimental import pallas as pl
from jax.experimental.pallas import tpu as pltpu

D_MODEL = 1024
CHUNK = 64
ATTN_WIDTH = 512
HGRN_WIDTH = 512
ATTN_HEAD_DIM = 64
ATTN_HEADS = 8
HGRN_HEAD_DIM = 128
HGRN_HEADS = 4
LEFT_CHUNKS = 8
PAD = LEFT_CHUNKS * CHUNK
REL_CLIP = 128
D_FF = 2816
RMS_EPS = 1e-6
PROJ_GROUPS = 7
PROJ_COLS = PROJ_GROUPS * ATTN_WIDTH

LANES = 128
SUBLANES = 8
BF16_ROWS = 2 * SUBLANES
MXU_TILE = 256
FF_TILE = MXU_TILE
ROW_TILE = 1024
FFN_ROW_PIECES = 4
LOG2E = float(np.log2(np.e))
ATTN_Q_CHUNKS = 2
ATTN_TQ = ATTN_Q_CHUNKS * CHUNK
ATTN_WIN = PAD + ATTN_TQ
MIX_ROWS = 512
N_LEVELS = 6
LEVELS = tuple(CHUNK >> (lv + 1) for lv in range(N_LEVELS))
FINE_LEVELS = tuple(c for c in LEVELS if c < SUBLANES)
VMEM_LIMIT = 56 * 1024 * 1024

BF16 = jnp.bfloat16
F32 = jnp.float32


def _dot(a, b):
    return jnp.dot(a, b, preferred_element_type=F32)


def _dot_nt(a, b):
    return lax.dot_general(a, b, (((1,), (1,)), ((), ())), preferred_element_type=F32)


def _dot_tn(a, b):
    return lax.dot_general(a, b, (((0,), (0,)), ((), ())), preferred_element_type=F32)


def _silu(x):
    return x / (1.0 + jnp.exp(-x))


def _rms(x, g):
    ms = jnp.mean(x * x, axis=-1, keepdims=True)
    return x * lax.rsqrt(ms + RMS_EPS) * g


def _resident(shape):
    return pl.BlockSpec(shape, lambda *_: (0,) * len(shape), pipeline_mode=pl.Buffered(1))


ROW_PIECES = [slice(r * ROW_TILE // FFN_ROW_PIECES, (r + 1) * ROW_TILE // FFN_ROW_PIECES)
              for r in range(FFN_ROW_PIECES)]


def _ffn_body(xs, g_ref, wg_ref, wu_ref, wd_ref, o_ref, act_ref):
    first = slice(0, FF_TILE)
    hs, gates, ups = [], [], []
    for xr in xs:
        hs.append(_rms(xr, g_ref[...]).astype(BF16))
        gates.append(_dot(hs[-1], wg_ref[:, first]))
        ups.append(_dot(hs[-1], wu_ref[:, first]))
    gate, up = jnp.concatenate(gates, axis=0), jnp.concatenate(ups, axis=0)
    act_ref[:, first] = (_silu(gate) * up).astype(BF16)
    h = jnp.concatenate(hs, axis=0)
    for j in range(1, D_FF // FF_TILE):
        cols = slice(j * FF_TILE, (j + 1) * FF_TILE)
        gate = _dot(h, wg_ref[:, cols])
        up = _dot(h, wu_ref[:, cols])
        act_ref[:, cols] = (_silu(gate) * up).astype(BF16)
    y = _dot(act_ref[...], wd_ref[...])
    o_ref[...] = jnp.concatenate(xs, axis=0) + 0.5 * y


def _ffn_kernel(n_cast, x_ref, g_ref, wg_ref, wu_ref, wd_ref, *rest):
    cast_in, (o_ref, *cast_out, act_ref) = rest[:n_cast], rest[n_cast:]
    for src, dst in zip(cast_in, cast_out):
        dst[...] = src[...].astype(BF16)
    _ffn_body([x_ref[rows, :] for rows in ROW_PIECES], g_ref, wg_ref, wu_ref, wd_ref, o_ref, act_ref)


def _mix_ffn_kernel(x_ref, m_ref, wo_ref, g_ref, wg_ref, wu_ref, wd_ref, o_ref, act_ref):
    xs = [x_ref[rows, :] + _dot(m_ref[rows, :], wo_ref[...]) for rows in ROW_PIECES]
    _ffn_body(xs, g_ref, wg_ref, wu_ref, wd_ref, o_ref, act_ref)


def _rows(width):
    return pl.BlockSpec((ROW_TILE, width), lambda i: (i, 0))


def _slab_spec(shape, steps):
    rows, width = shape
    per = 1 if (rows // steps) % BF16_ROWS == 0 and rows % steps == 0 else 2
    slab = rows * per // steps
    assert slab * steps == rows * per and slab % BF16_ROWS == 0
    return pl.BlockSpec((slab, width), lambda i: (i // per, 0))


def _ffn_call(x, g, wg, wu, wd, mix=None, cast=()):
    m = x.shape[0]
    steps = m // ROW_TILE
    w_specs = [_resident((1, D_MODEL)), _resident((D_MODEL, D_FF)),
               _resident((D_MODEL, D_FF)), _resident((D_FF, D_MODEL))]
    out_specs, out_shape = [_rows(D_MODEL)], [jax.ShapeDtypeStruct((m, D_MODEL), F32)]
    if mix is None:
        kern = functools.partial(_ffn_kernel, len(cast))
        ins = (x, g, wg, wu, wd) + tuple(cast)
        specs = [_rows(D_MODEL)] + w_specs + [_slab_spec(w.shape, steps) for w in cast]
        out_specs += [_slab_spec(w.shape, steps) for w in cast]
        out_shape += [jax.ShapeDtypeStruct(w.shape, BF16) for w in cast]
    else:
        assert not cast
        mixed, wo = mix
        kern = _mix_ffn_kernel
        ins = (x, mixed, wo, g, wg, wu, wd)
        specs = [_rows(D_MODEL), _rows(D_MODEL), _resident((D_MODEL, D_MODEL))] + w_specs
    return pl.pallas_call(
        kern,
        grid=(steps,),
        in_specs=specs,
        out_specs=out_specs,
        out_shape=out_shape,
        scratch_shapes=[pltpu.VMEM((ROW_TILE, D_FF), BF16)],
        compiler_params=pltpu.CompilerParams(
            dimension_semantics=("arbitrary",), vmem_limit_bytes=VMEM_LIMIT),
        name="mix_ffn" if mix is not None else "ffn",
    )(*ins)


def _projection_stages(x_ref, g_ref, w_ref, qg_ref, kg_ref, lbp_ref, seg_ref, h_ref, dst):
    names = ("q", "k", "v", "rq", "forget", "rv", "sg")
    per_group = ATTN_WIDTH // MXU_TILE

    def normalize():
        h_ref[...] = _rms(x_ref[0], g_ref[...]).astype(BF16)

    def project(i):
        return _dot(h_ref[...], w_ref[:, i * MXU_TILE:(i + 1) * MXU_TILE])

    def head_normed(a, gain):
        ms = _dot((a * a).astype(BF16), seg_ref[...]) * (1.0 / ATTN_HEAD_DIM)
        return (a * lax.rsqrt(ms + RMS_EPS) * gain).astype(BF16)

    def tail(i, val):
        group, part = divmod(i, per_group)
        cols = slice(part * MXU_TILE, (part + 1) * MXU_TILE)
        name = names[group]
        if name == "q":
            dst["q"](cols, head_normed(val, qg_ref[:, cols] * (ATTN_HEAD_DIM ** -0.5 * LOG2E)))
        elif name == "k":
            dst["k"](cols, head_normed(val, kg_ref[:, cols]))
        elif name in ("v", "rv"):
            dst[name](cols, val.astype(BF16))
        elif name in ("rq", "sg"):
            dst[name](cols, _silu(val).astype(BF16))
        else:
            lbp = lbp_ref[:, cols]
            e = jnp.exp(lbp - jnp.max(lbp, axis=0, keepdims=True))
            lb = e[0:1, :] / jnp.sum(e, axis=0, keepdims=True)
            en = jnp.exp(-jnp.abs(val))
            big, small = 1.0 / (1.0 + en), en / (1.0 + en)
            pos = val >= 0
            f = lb + (1.0 - lb) * jnp.where(pos, big, small)
            dst["lf"](cols, jnp.log2(f).astype(BF16))
            dst["rk"](cols, ((1.0 - lb) * jnp.where(pos, small, big)).astype(BF16))

    return PROJ_GROUPS * per_group, normalize, (project, tail)


def _attention_stages(tile, src, bias_ref, o_ref):
    n_pairs = ATTN_HEADS // 2
    groups = MIX_ROWS // ATTN_TQ

    def where(u):
        g, p = divmod(u, n_pairs)
        group = tile * groups + g
        window = pl.ds(pl.multiple_of(group * ATTN_TQ, ATTN_TQ), ATTN_WIN)
        return group, window, slice(g * ATTN_TQ, (g + 1) * ATTN_TQ), slice(p * LANES, (p + 1) * LANES), p

    def scores(u):
        group, window, rows, cols, p = where(u)
        qt = src["qt"](cols, rows)
        zero = jnp.zeros((ATTN_HEAD_DIM, ATTN_TQ), BF16)
        q2 = jnp.concatenate([jnp.concatenate([qt[:ATTN_HEAD_DIM], zero], axis=0),
                              jnp.concatenate([zero, qt[ATTN_HEAD_DIM:]], axis=0)], axis=1)
        st = _dot(src["k"](window, cols), q2)
        first_valid = PAD // ATTN_TQ - group
        blocks = []
        for j in range(ATTN_WIN // ATTN_TQ):
            sj = st[j * ATTN_TQ:(j + 1) * ATTN_TQ]
            if j in BIAS_BLOCKS:
                sj = sj + bias_ref[p, BIAS_BLOCKS.index(j)]
            blocks.append(jnp.where(j >= first_valid, sj, -jnp.inf))
        return jnp.concatenate(blocks, axis=0)

    def weigh(u, st):
        _, window, rows, cols, _ = where(u)
        e = jnp.exp2(st - jnp.max(st, axis=0, keepdims=True)).astype(BF16)
        vt1 = jnp.concatenate([src["vt"](cols, window), src["ones"](window)], axis=0)
        ot = _dot(vt1, e)
        ot = ot[:LANES] / ot[LANES:LANES + 1]
        same_head = jnp.concatenate([ot[:ATTN_HEAD_DIM, :ATTN_TQ], ot[ATTN_HEAD_DIM:, ATTN_TQ:]], axis=0)
        o_ref[0, rows, cols] = same_head.T.astype(BF16)

    return groups * n_pairs, (scores, weigh)


def _level_sum_matrix():
    u = np.arange(CHUNK)[:, None]
    r = np.arange(CHUNK)[None, :]
    mats = [(r <= u)]
    for c in FINE_LEVELS:
        m = (u // (2 * c)) * (2 * c) + c
        upper = (u & c) != 0
        mats.append(np.where(upper, (r > m) & (r <= u), (r > u) & (r <= m)))
    return np.concatenate(mats, axis=0).astype(np.float32)


def _level_masks():
    t = np.arange(CHUNK)[:, None]
    s = np.arange(CHUNK)[None, :]
    return np.stack([t == s] + [(t ^ s) < 2 * c for c in LEVELS]).astype(np.float32)


def _hgrn_stages(src, lsum_ref, lmask_ref, og_ref, o_ref, state_ref):
    t_row = lax.broadcasted_iota(jnp.int32, (CHUNK, HGRN_HEAD_DIM), 0)
    esums = {}

    def where(n):
        u, hd = divmod(n, HGRN_HEADS)
        return u, slice(u * CHUNK, (u + 1) * CHUNK), slice(hd * HGRN_HEAD_DIM, (hd + 1) * HGRN_HEAD_DIM), hd

    def coarse_operands(q, k, b, c):
        qs, ks = [], []
        zero = jnp.zeros((c, HGRN_HEAD_DIM), F32)
        for blk in range(CHUNK // c):
            r = slice(blk * c, (blk + 1) * c)
            m = (blk // 2) * 2 * c + c
            b_m = b[m:m + 1, :]
            qs.append(q[r] * jnp.exp2(b[r] - b_m) if blk % 2 else zero)
            ks.append(zero if blk % 2 else k[r] * jnp.exp2(b_m - b[r]))
        return jnp.concatenate(qs, axis=0).astype(BF16), jnp.concatenate(ks, axis=0).astype(BF16)

    def fine_operands(q, k, w, c):
        upper = (t_row & c) != 0
        return (jnp.where(upper, q * w, 0.0).astype(BF16), jnp.where(upper, 0.0, k * w).astype(BF16))

    def products(n):
        u, rows, cols, _ = where(n)
        if u not in esums:
            esums[u] = _dot(lsum_ref[...], src["lf"](rows, slice(None)))
        esum = esums[u]
        q = src["rq"](rows, cols).astype(F32)
        k = src["rk"](rows, cols).astype(F32)
        b = esum[:CHUNK, cols]
        b_last = b[CHUNK - 1:CHUNK, :]
        prods = []
        for c in LEVELS:
            if c in FINE_LEVELS:
                i = 1 + FINE_LEVELS.index(c)
                qw, kw = fine_operands(q, k, jnp.exp2(esum[i * CHUNK:(i + 1) * CHUNK, cols]), c)
            else:
                qw, kw = coarse_operands(q, k, b, c)
            prods.append(_dot_nt(qw, kw))
        return dict(q_in=(q * jnp.exp2(b)).astype(BF16), k_up=(k * jnp.exp2(b_last - b)).astype(BF16),
                    decay=jnp.exp2(b_last), diag=jnp.sum(q * k, axis=-1, keepdims=True), prods=prods)

    def finish(n, a):
        _, rows, cols, hd = where(n)
        scores = a["diag"] * lmask_ref[0]
        for lv in range(N_LEVELS):
            scores = scores + a["prods"][lv] * lmask_ref[1 + lv]
        v = src["rv"](rows, cols)
        state = state_ref[hd]
        o = _dot_nt(a["q_in"], state.astype(BF16)) + _dot(scores.astype(BF16), v)
        state_ref[hd] = state * a["decay"] + _dot_tn(v, a["k_up"])
        y = _rms(o, og_ref[...]) * src["sg"](rows, cols).astype(F32)
        out_cols = slice(ATTN_WIDTH + hd * HGRN_HEAD_DIM, ATTN_WIDTH + (hd + 1) * HGRN_HEAD_DIM)
        o_ref[0, rows, out_cols] = y.astype(BF16)

    return (MIX_ROWS // CHUNK) * HGRN_HEADS, (products, finish)


def _mixer_kernel(x0_ref, xn_ref, g_ref, w_ref, qg_ref, kg_ref, lbp_ref, seg_ref, bias_ref, lsum_ref,
                  lmask_ref, og_ref, o_ref, h_ref, qt_ref, rq_ref, rk_ref, rv_ref, sg_ref, lf_ref,
                  kpad_ref, vt_ref, ones_ref, state_ref):
    seq = kpad_ref.shape[1] - PAD
    tiles = seq // MIX_ROWS
    s = pl.program_id(0)
    tile = lax.rem(s, tiles)
    nxt = jnp.minimum(s + 1, pl.num_programs(0) - 1)

    def projection(x_ref, step):
        slot = lax.rem(step, 2)
        par = lax.rem(lax.div(step, tiles), 2)
        rows = pl.ds(pl.multiple_of(PAD + lax.rem(step, tiles) * MIX_ROWS, MIX_ROWS), MIX_ROWS)

        def put(ref):
            def store(cols, val):
                ref[slot, :, cols] = val
            return store

        def put_k(cols, val):
            kpad_ref[par, rows, cols] = val

        def put_v(cols, val):
            vt_ref[par, cols, rows] = val.T

        def put_q(cols, val):
            qt_ref[slot, cols, :] = val.T

        dst = dict(q=put_q, k=put_k, v=put_v, rq=put(rq_ref), rk=put(rk_ref), rv=put(rv_ref),
                   sg=put(sg_ref), lf=put(lf_ref))
        return _projection_stages(x_ref, g_ref, w_ref, qg_ref, kg_ref, lbp_ref, seg_ref, h_ref, dst)

    @pl.when(s == 0)
    def _():
        kpad_ref[:, :PAD, :] = jnp.zeros((2, PAD, ATTN_WIDTH), BF16)
        vt_ref[:, :, :PAD] = jnp.zeros((2, ATTN_WIDTH, PAD), BF16)
        ones_ref[:, :PAD] = jnp.zeros((BF16_ROWS, PAD), BF16)
        ones_ref[:, PAD:] = jnp.ones((BF16_ROWS, seq), BF16)
        n_units, normalize, (project, tail) = projection(x0_ref, s)
        normalize()
        nxt_val = project(0)
        for i in range(n_units):
            cur, nxt_val = nxt_val, (project(i + 1) if i + 1 < n_units else None)
            tail(i, cur)

    @pl.when(tile == 0)
    def _():
        state_ref[...] = jnp.zeros_like(state_ref)

    slot = lax.rem(s, 2)
    par = lax.rem(lax.div(s, tiles), 2)
    tile_src = lambda ref: (lambda rows, cols: ref[slot, rows, cols])
    src = dict(qt=tile_src(qt_ref), rq=tile_src(rq_ref), rk=tile_src(rk_ref), rv=tile_src(rv_ref),
               sg=tile_src(sg_ref), lf=tile_src(lf_ref),
               k=lambda window, cols: kpad_ref[par, window, cols],
               vt=lambda cols, window: vt_ref[par, cols, window],
               ones=lambda window: ones_ref[:, window])
    n_proj, normalize, (proj_first, proj_second) = projection(xn_ref, nxt)
    n_attn, (attn_first, attn_second) = _attention_stages(tile, src, bias_ref, o_ref)
    n_hgrn, (hgrn_first, hgrn_second) = _hgrn_stages(src, lsum_ref, lmask_ref, og_ref, o_ref, state_ref)

    assert n_proj <= n_attn and n_hgrn % n_attn == 0
    per = n_hgrn // n_attn
    normalize()
    p_next, a_next, h_next = proj_first(0), attn_first(0), hgrn_first(0)
    for i in range(n_attn):
        a_cur, a_next = a_next, (attn_first(i + 1) if i + 1 < n_attn else None)
        for n in range(i * per, (i + 1) * per):
            h_cur, h_next = h_next, (hgrn_first(n + 1) if n + 1 < n_hgrn else None)
            hgrn_second(n, h_cur)
            if n == i * per:
                attn_second(i, a_cur)
                if i < n_proj:
                    p_cur, p_next = p_next, (proj_first(i + 1) if i + 1 < n_proj else None)
                    proj_second(i, p_cur)


def _mixer_call(x1, g, w_in, qg, kg, lbp, seg, bias, lsum, lmask, og):
    b, s, _ = x1.shape
    tiles = s // MIX_ROWS
    steps = b * tiles
    slot_pair = pltpu.VMEM((2, MIX_ROWS, HGRN_WIDTH), BF16)

    def next_tile(i):
        n = jnp.minimum(i + 1, steps - 1)
        return (n // tiles, n % tiles, 0)

    return pl.pallas_call(
        _mixer_kernel,
        grid=(steps,),
        in_specs=[pl.BlockSpec((1, MIX_ROWS, D_MODEL), lambda i: (0, 0, 0), pipeline_mode=pl.Buffered(1)),
                  pl.BlockSpec((1, MIX_ROWS, D_MODEL), next_tile),
                  _resident((1, D_MODEL)), _resident((D_MODEL, PROJ_COLS)),
                  _resident((1, ATTN_WIDTH)), _resident((1, ATTN_WIDTH)), _resident(lbp.shape),
                  _resident((MXU_TILE, MXU_TILE)), _resident(bias.shape), _resident(lsum.shape),
                  _resident(lmask.shape), _resident((1, HGRN_HEAD_DIM))],
        out_specs=pl.BlockSpec((1, MIX_ROWS, D_MODEL), lambda i: (i // tiles, i % tiles, 0)),
        out_shape=jax.ShapeDtypeStruct((b, s, D_MODEL), BF16),
        scratch_shapes=[pltpu.VMEM((MIX_ROWS, D_MODEL), BF16),
                        pltpu.VMEM((2, ATTN_WIDTH, MIX_ROWS), BF16)] + [slot_pair] * 5 + [
                        pltpu.VMEM((2, s + PAD, ATTN_WIDTH), BF16),
                        pltpu.VMEM((2, ATTN_WIDTH, s + PAD), BF16),
                        pltpu.VMEM((BF16_ROWS, s + PAD), BF16),
                        pltpu.VMEM((HGRN_HEADS, HGRN_HEAD_DIM, HGRN_HEAD_DIM), F32)],
        compiler_params=pltpu.CompilerParams(
            dimension_semantics=("arbitrary",), vmem_limit_bytes=VMEM_LIMIT),
        name="mixer",
    )(x1, x1, g, w_in, qg, kg, lbp, seg, bias, lsum, lmask, og)


def _bias_blocks():
    qq = np.arange(ATTN_TQ)[:, None]
    key = np.arange(ATTN_WIN)[None, :]
    near = PAD + qq - key < REL_CLIP
    band = (key // CHUNK >= qq // CHUNK) & (key // CHUNK <= qq // CHUNK + LEFT_CHUNKS)
    used = (near | ~band).any(axis=0).reshape(ATTN_WIN // ATTN_TQ, ATTN_TQ).any(axis=1)
    return tuple(int(j) for j in np.nonzero(used)[0])


BIAS_BLOCKS = _bias_blocks()


def _rel_bias_table(rel_bias):
    assert ATTN_TQ - 1 <= REL_CLIP
    rb = (rel_bias.astype(F32) - rel_bias.astype(F32)[:, 2 * REL_CLIP:]) * LOG2E
    n = ATTN_WIN + ATTN_TQ
    far = jnp.broadcast_to(rb[:, 2 * REL_CLIP:], (ATTN_HEADS, ATTN_WIN - REL_CLIP))
    near = rb[:, REL_CLIP - ATTN_TQ + 1:2 * REL_CLIP][:, ::-1]
    x = jnp.concatenate([far, near, jnp.zeros((ATTN_HEADS, 1), F32)], axis=1)
    rolled = jnp.tile(x, (1, ATTN_TQ))[:, :ATTN_TQ * (n - 1)].reshape(ATTN_HEADS, ATTN_TQ, n - 1)
    tbl = rolled[:, :, ATTN_TQ - 1:ATTN_TQ - 1 + ATTN_WIN]
    qc = np.arange(ATTN_TQ)[:, None] // CHUNK
    kc = np.arange(ATTN_WIN)[None, :] // CHUNK
    tbl = jnp.where((kc >= qc) & (kc <= qc + LEFT_CHUNKS), tbl, -jnp.inf)
    tbl = tbl.reshape(ATTN_HEADS // 2, 2, ATTN_TQ, ATTN_WIN).transpose(0, 3, 1, 2)
    tbl = tbl.reshape(ATTN_HEADS // 2, ATTN_WIN // ATTN_TQ, ATTN_TQ, 2 * ATTN_TQ)
    return tbl[:, np.asarray(BIAS_BLOCKS)]


def kernel(x, ffn1_norm_g, ffn1_w_gate, ffn1_w_up, ffn1_w_down, mix_norm_g, w_in,
           attn_q_norm_g, attn_k_norm_g, attn_rel_bias, hgrn_lower_bounds, hgrn_out_norm_g,
           w_out, ffn2_norm_g, ffn2_w_gate, ffn2_w_up, ffn2_w_down):
    bsz, seq, _ = x.shape
    depth = ffn1_norm_g.shape[0]
    assert depth == 1 and seq % MIX_ROWS == 0 and (bsz * seq) % ROW_TILE == 0
    head_of_col = np.arange(MXU_TILE) // ATTN_HEAD_DIM
    seg = jnp.asarray(head_of_col[:, None] == head_of_col[None, :], dtype=BF16)
    lsum = jnp.asarray(_level_sum_matrix(), dtype=BF16)
    lmask = jnp.asarray(_level_masks())
    row = lambda g: g.reshape(1, -1).astype(F32)
    tile_heads = lambda g: jnp.tile(g.astype(F32), ATTN_HEADS).reshape(1, ATTN_WIDTH)

    xf = x.reshape(bsz * seq, D_MODEL)
    for l in range(depth):
        later = (w_in[l], w_out[l], ffn2_w_gate[l], ffn2_w_up[l], ffn2_w_down[l])
        x1, w_in_b, w_out_b, wg2_b, wu2_b, wd2_b = _ffn_call(
            xf, row(ffn1_norm_g[l]), ffn1_w_gate[l].astype(BF16), ffn1_w_up[l].astype(BF16),
            ffn1_w_down[l].astype(BF16), cast=tuple(w.astype(F32) for w in later))
        mixed = _mixer_call(
            x1.reshape(bsz, seq, D_MODEL), row(mix_norm_g[l]), w_in_b,
            tile_heads(attn_q_norm_g[l]), tile_heads(attn_k_norm_g[l]), hgrn_lower_bounds.astype(F32),
            seg, _rel_bias_table(attn_rel_bias[l]), lsum, lmask, row(hgrn_out_norm_g[l]))
        xf, = _ffn_call(x1, row(ffn2_norm_g[l]), wg2_b, wu2_b, wd2_b,
                        mix=(mixed.reshape(bsz * seq, D_MODEL), w_out_b))
    return xf.reshape(bsz, seq, D_MODEL)
```

```python
import functools

import numpy as np
import jax
import jax.numpy as jnp
from jax import lax
from jax.experimental import pallas as pl
from jax.experimental.pallas import tpu as pltpu

D_MODEL = 1024
CHUNK = 64
ATTN_WIDTH = 512
HGRN_WIDTH = 512
ATTN_HEAD_DIM = 64
ATTN_HEADS = 8
HGRN_HEAD_DIM = 128
HGRN_HEADS = 4
LEFT_CHUNKS = 8
PAD = LEFT_CHUNKS * CHUNK
REL_CLIP = 128
D_FF = 2816
RMS_EPS = 1e-6
PROJ_GROUPS = 7
PROJ_COLS = PROJ_GROUPS * ATTN_WIDTH

LANES = 128
SUBLANES = 8
BF16_ROWS = 2 * SUBLANES
MXU_TILE = 256
FF_TILE = MXU_TILE
ROW_TILE = 1024
FFN_ROW_PIECES = 4
LOG2E = float(np.log2(np.e))
ATTN_Q_CHUNKS = 2
ATTN_TQ = ATTN_Q_CHUNKS * CHUNK
ATTN_WIN = PAD + ATTN_TQ
MIX_ROWS = 512
N_LEVELS = 6
LEVELS = tuple(CHUNK >> (lv + 1) for lv in range(N_LEVELS))
FINE_LEVELS = tuple(c for c in LEVELS if c < SUBLANES)
VMEM_LIMIT = 56 * 1024 * 1024

BF16 = jnp.bfloat16
F32 = jnp.float32


def _dot(a, b):
    return jnp.dot(a, b, preferred_element_type=F32)


def _dot_nt(a, b):
    return lax.dot_general(a, b, (((1,), (1,)), ((), ())), preferred_element_type=F32)


def _dot_tn(a, b):
    return lax.dot_general(a, b, (((0,), (0,)), ((), ())), preferred_element_type=F32)


def _silu(x):
    return x / (1.0 + jnp.exp(-x))


def _rms(x, g):
    ms = jnp.mean(x * x, axis=-1, keepdims=True)
    return x * lax.rsqrt(ms + RMS_EPS) * g


def _resident(shape):
    return pl.BlockSpec(shape, lambda *_: (0,) * len(shape), pipeline_mode=pl.Buffered(1))


ROW_PIECES = [slice(r * ROW_TILE // FFN_ROW_PIECES, (r + 1) * ROW_TILE // FFN_ROW_PIECES)
              for r in range(FFN_ROW_PIECES)]


def _ffn_body(xs, g_ref, wg_ref, wu_ref, wd_ref, o_ref, act_ref):
    first = slice(0, FF_TILE)
    hs, gates, ups = [], [], []
    for xr in xs:
        hs.append(_rms(xr, g_ref[...]).astype(BF16))
        gates.append(_dot(hs[-1], wg_ref[:, first]))
        ups.append(_dot(hs[-1], wu_ref[:, first]))
    gate, up = jnp.concatenate(gates, axis=0), jnp.concatenate(ups, axis=0)
    act_ref[:, first] = (_silu(gate) * up).astype(BF16)
    h = jnp.concatenate(hs, axis=0)
    for j in range(1, D_FF // FF_TILE):
        cols = slice(j * FF_TILE, (j + 1) * FF_TILE)
        gate = _dot(h, wg_ref[:, cols])
        up = _dot(h, wu_ref[:, cols])
        act_ref[:, cols] = (_silu(gate) * up).astype(BF16)
    y = _dot(act_ref[...], wd_ref[...])
    o_ref[...] = jnp.concatenate(xs, axis=0) + 0.5 * y


def _ffn_kernel(n_cast, x_ref, g_ref, wg_ref, wu_ref, wd_ref, *rest):
    cast_in, (o_ref, *cast_out, act_ref) = rest[:n_cast], rest[n_cast:]
    for src, dst in zip(cast_in, cast_out):
        dst[...] = src[...].astype(BF16)
    _ffn_body([x_ref[rows, :] for rows in ROW_PIECES], g_ref, wg_ref, wu_ref, wd_ref, o_ref, act_ref)


def _mix_ffn_kernel(x_ref, m_ref, wo_ref, g_ref, wg_ref, wu_ref, wd_ref, o_ref, act_ref):
    xs = [x_ref[rows, :] + _dot(m_ref[rows, :], wo_ref[...]) for rows in ROW_PIECES]
    _ffn_body(xs, g_ref, wg_ref, wu_ref, wd_ref, o_ref, act_ref)


def _rows(width):
    return pl.BlockSpec((ROW_TILE, width), lambda i: (i, 0))


def _slab_spec(shape, steps):
    rows, width = shape
    per = 1 if (rows // steps) % BF16_ROWS == 0 and rows % steps == 0 else 2
    slab = rows * per // steps
    assert slab * steps == rows * per and slab % BF16_ROWS == 0
    return pl.BlockSpec((slab, width), lambda i: (i // per, 0))


def _ffn_call(x, g, wg, wu, wd, mix=None, cast=()):
    m = x.shape[0]
    steps = m // ROW_TILE
    w_specs = [_resident((1, D_MODEL)), _resident((D_MODEL, D_FF)),
               _resident((D_MODEL, D_FF)), _resident((D_FF, D_MODEL))]
    out_specs, out_shape = [_rows(D_MODEL)], [jax.ShapeDtypeStruct((m, D_MODEL), F32)]
    if mix is None:
        kern = functools.partial(_ffn_kernel, len(cast))
        ins = (x, g, wg, wu, wd) + tuple(cast)
        specs = [_rows(D_MODEL)] + w_specs + [_slab_spec(w.shape, steps) for w in cast]
        out_specs += [_slab_spec(w.shape, steps) for w in cast]
        out_shape += [jax.ShapeDtypeStruct(w.shape, BF16) for w in cast]
    else:
        assert not cast
        mixed, wo = mix
        kern = _mix_ffn_kernel
        ins = (x, mixed, wo, g, wg, wu, wd)
        specs = [_rows(D_MODEL), _rows(D_MODEL), _resident((D_MODEL, D_MODEL))] + w_specs
    return pl.pallas_call(
        kern,
        grid=(steps,),
        in_specs=specs,
        out_specs=out_specs,
        out_shape=out_shape,
        scratch_shapes=[pltpu.VMEM((ROW_TILE, D_FF), BF16)],
        compiler_params=pltpu.CompilerParams(
            dimension_semantics=("arbitrary",), vmem_limit_bytes=VMEM_LIMIT),
        name="mix_ffn" if mix is not None else "ffn",
    )(*ins)


def _projection_stages(x_ref, g_ref, w_ref, qg_ref, kg_ref, lbp_ref, seg_ref, h_ref, dst):
    names = ("q", "k", "v", "rq", "forget", "rv", "sg")
    per_group = ATTN_WIDTH // MXU_TILE

    def normalize():
        h_ref[...] = _rms(x_ref[0], g_ref[...]).astype(BF16)

    def project(i):
        return _dot(h_ref[...], w_ref[:, i * MXU_TILE:(i + 1) * MXU_TILE])

    def head_normed(a, gain):
        ms = _dot((a * a).astype(BF16), seg_ref[...]) * (1.0 / ATTN_HEAD_DIM)
        return (a * lax.rsqrt(ms + RMS_EPS) * gain).astype(BF16)

    def tail(i, val):
        group, part = divmod(i, per_group)
        cols = slice(part * MXU_TILE, (part + 1) * MXU_TILE)
        name = names[group]
        if name == "q":
            dst["q"](cols, head_normed(val, qg_ref[:, cols] * (ATTN_HEAD_DIM ** -0.5 * LOG2E)))
        elif name == "k":
            dst["k"](cols, head_normed(val, kg_ref[:, cols]))
        elif name in ("v", "rv"):
            dst[name](cols, val.astype(BF16))
        elif name in ("rq", "sg"):
            dst[name](cols, _silu(val).astype(BF16))
        else:
            lbp = lbp_ref[:, cols]
            e = jnp.exp(lbp - jnp.max(lbp, axis=0, keepdims=True))
            lb = e[0:1, :] / jnp.sum(e, axis=0, keepdims=True)
            en = jnp.exp(-jnp.abs(val))
            big, small = 1.0 / (1.0 + en), en / (1.0 + en)
            pos = val >= 0
            f = lb + (1.0 - lb) * jnp.where(pos, big, small)
            dst["lf"](cols, jnp.log2(f).astype(BF16))
            dst["rk"](cols, ((1.0 - lb) * jnp.where(pos, small, big)).astype(BF16))

    return PROJ_GROUPS * per_group, normalize, (project, tail)


def _attention_stages(tile, src, bias_ref, o_ref):
    n_pairs = ATTN_HEADS // 2
    groups = MIX_ROWS // ATTN_TQ

    def where(u):
        g, p = divmod(u, n_pairs)
        group = tile * groups + g
        window = pl.ds(pl.multiple_of(group * ATTN_TQ, ATTN_TQ), ATTN_WIN)
        return group, window, slice(g * ATTN_TQ, (g + 1) * ATTN_TQ), slice(p * LANES, (p + 1) * LANES), p

    def scores(u):
        group, window, rows, cols, p = where(u)
        qt = src["qt"](cols, rows)
        zero = jnp.zeros((ATTN_HEAD_DIM, ATTN_TQ), BF16)
        q2 = jnp.concatenate([jnp.concatenate([qt[:ATTN_HEAD_DIM], zero], axis=0),
                              jnp.concatenate([zero, qt[ATTN_HEAD_DIM:]], axis=0)], axis=1)
        st = _dot(src["k"](window, cols), q2)
        first_valid = PAD // ATTN_TQ - group
        blocks = []
        for j in range(ATTN_WIN // ATTN_TQ):
            sj = st[j * ATTN_TQ:(j + 1) * ATTN_TQ]
            if j in BIAS_BLOCKS:
                sj = sj + bias_ref[p, BIAS_BLOCKS.index(j)]
            blocks.append(jnp.where(j >= first_valid, sj, -jnp.inf))
        return jnp.concatenate(blocks, axis=0)

    def weigh(u, st):
        _, window, rows, cols, _ = where(u)
        e = jnp.exp2(st - jnp.max(st, axis=0, keepdims=True)).astype(BF16)
        vt1 = jnp.concatenate([src["vt"](cols, window), src["ones"](window)], axis=0)
        ot = _dot(vt1, e)
        ot = ot[:LANES] / ot[LANES:LANES + 1]
        same_head = jnp.concatenate([ot[:ATTN_HEAD_DIM, :ATTN_TQ], ot[ATTN_HEAD_DIM:, ATTN_TQ:]], axis=0)
        o_ref[0, rows, cols] = same_head.T.astype(BF16)

    return groups * n_pairs, (scores, weigh)


def _level_sum_matrix():
    u = np.arange(CHUNK)[:, None]
    r = np.arange(CHUNK)[None, :]
    mats = [(r <= u)]
    for c in FINE_LEVELS:
        m = (u // (2 * c)) * (2 * c) + c
        upper = (u & c) != 0
        mats.append(np.where(upper, (r > m) & (r <= u), (r > u) & (r <= m)))
    return np.concatenate(mats, axis=0).astype(np.float32)


def _level_masks():
    t = np.arange(CHUNK)[:, None]
    s = np.arange(CHUNK)[None, :]
    return np.stack([t == s] + [(t ^ s) < 2 * c for c in LEVELS]).astype(np.float32)


def _hgrn_stages(src, lsum_ref, lmask_ref, og_ref, o_ref, state_ref):
    t_row = lax.broadcasted_iota(jnp.int32, (CHUNK, HGRN_HEAD_DIM), 0)
    esums = {}

    def where(n):
        u, hd = divmod(n, HGRN_HEADS)
        return u, slice(u * CHUNK, (u + 1) * CHUNK), slice(hd * HGRN_HEAD_DIM, (hd + 1) * HGRN_HEAD_DIM), hd

    def coarse_operands(q, k, b, c):
        qs, ks = [], []
        zero = jnp.zeros((c, HGRN_HEAD_DIM), F32)
        for blk in range(CHUNK // c):
            r = slice(blk * c, (blk + 1) * c)
            m = (blk // 2) * 2 * c + c
            b_m = b[m:m + 1, :]
            qs.append(q[r] * jnp.exp2(b[r] - b_m) if blk % 2 else zero)
            ks.append(zero if blk % 2 else k[r] * jnp.exp2(b_m - b[r]))
        return jnp.concatenate(qs, axis=0).astype(BF16), jnp.concatenate(ks, axis=0).astype(BF16)

    def fine_operands(q, k, w, c):
        upper = ((t_row & c) != 0).astype(F32)
        return ((q * (w * upper)).astype(BF16), (k * (w * (1.0 - upper))).astype(BF16))

    def products(n):
        u, rows, cols, _ = where(n)
        if u not in esums:
            esums[u] = _dot(lsum_ref[...], src["lf"](rows, slice(None)))
        esum = esums[u]
        q = src["rq"](rows, cols).astype(F32)
        k = src["rk"](rows, cols).astype(F32)
        b = esum[:CHUNK, cols]
        b_last = b[CHUNK - 1:CHUNK, :]
        prods = []
        for c in LEVELS:
            if c in FINE_LEVELS:
                i = 1 + FINE_LEVELS.index(c)
                qw, kw = fine_operands(q, k, jnp.exp2(esum[i * CHUNK:(i + 1) * CHUNK, cols]), c)
            else:
                qw, kw = coarse_operands(q, k, b, c)
            prods.append(_dot_nt(qw, kw))
        return dict(q_in=(q * jnp.exp2(b)).astype(BF16), k_up=(k * jnp.exp2(b_last - b)).astype(BF16),
                    decay=jnp.exp2(b_last), diag=jnp.sum(q * k, axis=-1, keepdims=True), prods=prods)

    def finish(n, a):
        _, rows, cols, hd = where(n)
        scores = a["diag"] * lmask_ref[0]
        for lv in range(N_LEVELS):
            scores = scores + a["prods"][lv] * lmask_ref[1 + lv]
        v = src["rv"](rows, cols)
        state = state_ref[hd]
        o = _dot_nt(a["q_in"], state.astype(BF16)) + _dot(scores.astype(BF16), v)
        state_ref[hd] = state * a["decay"] + _dot_tn(v, a["k_up"])
        y = _rms(o, og_ref[...]) * src["sg"](rows, cols).astype(F32)
        out_cols = slice(ATTN_WIDTH + hd * HGRN_HEAD_DIM, ATTN_WIDTH + (hd + 1) * HGRN_HEAD_DIM)
        o_ref[0, rows, out_cols] = y.astype(BF16)

    return (MIX_ROWS // CHUNK) * HGRN_HEADS, (products, finish)


def _mixer_kernel(x0_ref, xn_ref, g_ref, w_ref, qg_ref, kg_ref, lbp_ref, seg_ref, bias_ref, lsum_ref,
                  lmask_ref, og_ref, o_ref, h_ref, qt_ref, rq_ref, rk_ref, rv_ref, sg_ref, lf_ref,
                  kpad_ref, vt_ref, ones_ref, state_ref):
    seq = kpad_ref.shape[1] - PAD
    tiles = seq // MIX_ROWS
    s = pl.program_id(0)
    tile = lax.rem(s, tiles)
    nxt = jnp.minimum(s + 1, pl.num_programs(0) - 1)

    def projection(x_ref, step):
        slot = lax.rem(step, 2)
        par = lax.rem(lax.div(step, tiles), 2)
        rows = pl.ds(pl.multiple_of(PAD + lax.rem(step, tiles) * MIX_ROWS, MIX_ROWS), MIX_ROWS)

        def put(ref):
            def store(cols, val):
                ref[slot, :, cols] = val
            return store

        def put_k(cols, val):
            kpad_ref[par, rows, cols] = val

        def put_v(cols, val):
            vt_ref[par, cols, rows] = val.T

        def put_q(cols, val):
            qt_ref[slot, cols, :] = val.T

        dst = dict(q=put_q, k=put_k, v=put_v, rq=put(rq_ref), rk=put(rk_ref), rv=put(rv_ref),
                   sg=put(sg_ref), lf=put(lf_ref))
        return _projection_stages(x_ref, g_ref, w_ref, qg_ref, kg_ref, lbp_ref, seg_ref, h_ref, dst)

    @pl.when(s == 0)
    def _():
        kpad_ref[:, :PAD, :] = jnp.zeros((2, PAD, ATTN_WIDTH), BF16)
        vt_ref[:, :, :PAD] = jnp.zeros((2, ATTN_WIDTH, PAD), BF16)
        ones_ref[:, :PAD] = jnp.zeros((BF16_ROWS, PAD), BF16)
        ones_ref[:, PAD:] = jnp.ones((BF16_ROWS, seq), BF16)
        n_units, normalize, (project, tail) = projection(x0_ref, s)
        normalize()
        nxt_val = project(0)
        for i in range(n_units):
            cur, nxt_val = nxt_val, (project(i + 1) if i + 1 < n_units else None)
            tail(i, cur)

    @pl.when(tile == 0)
    def _():
        state_ref[...] = jnp.zeros_like(state_ref)

    slot = lax.rem(s, 2)
    par = lax.rem(lax.div(s, tiles), 2)
    tile_src = lambda ref: (lambda rows, cols: ref[slot, rows, cols])
    src = dict(qt=tile_src(qt_ref), rq=tile_src(rq_ref), rk=tile_src(rk_ref), rv=tile_src(rv_ref),
               sg=tile_src(sg_ref), lf=tile_src(lf_ref),
               k=lambda window, cols: kpad_ref[par, window, cols],
               vt=lambda cols, window: vt_ref[par, cols, window],
               ones=lambda window: ones_ref[:, window])
    n_proj, normalize, (proj_first, proj_second) = projection(xn_ref, nxt)
    n_attn, (attn_first, attn_second) = _attention_stages(tile, src, bias_ref, o_ref)
    n_hgrn, (hgrn_first, hgrn_second) = _hgrn_stages(src, lsum_ref, lmask_ref, og_ref, o_ref, state_ref)

    assert n_proj <= n_attn and n_hgrn % n_attn == 0
    per = n_hgrn // n_attn
    normalize()
    p_next, a_next, h_next = proj_first(0), attn_first(0), hgrn_first(0)
    for i in range(n_attn):
        a_cur, a_next = a_next, (attn_first(i + 1) if i + 1 < n_attn else None)
        for n in range(i * per, (i + 1) * per):
            h_cur, h_next = h_next, (hgrn_first(n + 1) if n + 1 < n_hgrn else None)
            hgrn_second(n, h_cur)
            if n == i * per:
                attn_second(i, a_cur)
                if i < n_proj:
                    p_cur, p_next = p_next, (proj_first(i + 1) if i + 1 < n_proj else None)
                    proj_second(i, p_cur)


def _mixer_call(x1, g, w_in, qg, kg, lbp, seg, bias, lsum, lmask, og):
    b, s, _ = x1.shape
    tiles = s // MIX_ROWS
    steps = b * tiles
    slot_pair = pltpu.VMEM((2, MIX_ROWS, HGRN_WIDTH), BF16)

    def next_tile(i):
        n = jnp.minimum(i + 1, steps - 1)
        return (n // tiles, n % tiles, 0)

    return pl.pallas_call(
        _mixer_kernel,
        grid=(steps,),
        in_specs=[pl.BlockSpec((1, MIX_ROWS, D_MODEL), lambda i: (0, 0, 0), pipeline_mode=pl.Buffered(1)),
                  pl.BlockSpec((1, MIX_ROWS, D_MODEL), next_tile),
                  _resident((1, D_MODEL)), _resident((D_MODEL, PROJ_COLS)),
                  _resident((1, ATTN_WIDTH)), _resident((1, ATTN_WIDTH)), _resident(lbp.shape),
                  _resident((MXU_TILE, MXU_TILE)), _resident(bias.shape), _resident(lsum.shape),
                  _resident(lmask.shape), _resident((1, HGRN_HEAD_DIM))],
        out_specs=pl.BlockSpec((1, MIX_ROWS, D_MODEL), lambda i: (i // tiles, i % tiles, 0)),
        out_shape=jax.ShapeDtypeStruct((b, s, D_MODEL), BF16),
        scratch_shapes=[pltpu.VMEM((MIX_ROWS, D_MODEL), BF16),
                        pltpu.VMEM((2, ATTN_WIDTH, MIX_ROWS), BF16)] + [slot_pair] * 5 + [
                        pltpu.VMEM((2, s + PAD, ATTN_WIDTH), BF16),
                        pltpu.VMEM((2, ATTN_WIDTH, s + PAD), BF16),
                        pltpu.VMEM((BF16_ROWS, s + PAD), BF16),
                        pltpu.VMEM((HGRN_HEADS, HGRN_HEAD_DIM, HGRN_HEAD_DIM), F32)],
        compiler_params=pltpu.CompilerParams(
            dimension_semantics=("arbitrary",), vmem_limit_bytes=VMEM_LIMIT),
        name="mixer",
    )(x1, x1, g, w_in, qg, kg, lbp, seg, bias, lsum, lmask, og)


def _bias_blocks():
    qq = np.arange(ATTN_TQ)[:, None]
    key = np.arange(ATTN_WIN)[None, :]
    near = PAD + qq - key < REL_CLIP
    band = (key // CHUNK >= qq // CHUNK) & (key // CHUNK <= qq // CHUNK + LEFT_CHUNKS)
    used = (near | ~band).any(axis=0).reshape(ATTN_WIN // ATTN_TQ, ATTN_TQ).any(axis=1)
    return tuple(int(j) for j in np.nonzero(used)[0])


BIAS_BLOCKS = _bias_blocks()


def _rel_bias_table(rel_bias):
    assert ATTN_TQ - 1 <= REL_CLIP
    rb = (rel_bias.astype(F32) - rel_bias.astype(F32)[:, 2 * REL_CLIP:]) * LOG2E
    n = ATTN_WIN + ATTN_TQ
    far = jnp.broadcast_to(rb[:, 2 * REL_CLIP:], (ATTN_HEADS, ATTN_WIN - REL_CLIP))
    near = rb[:, REL_CLIP - ATTN_TQ + 1:2 * REL_CLIP][:, ::-1]
    x = jnp.concatenate([far, near, jnp.zeros((ATTN_HEADS, 1), F32)], axis=1)
    rolled = jnp.tile(x, (1, ATTN_TQ))[:, :ATTN_TQ * (n - 1)].reshape(ATTN_HEADS, ATTN_TQ, n - 1)
    tbl = rolled[:, :, ATTN_TQ - 1:ATTN_TQ - 1 + ATTN_WIN]
    qc = np.arange(ATTN_TQ)[:, None] // CHUNK
    kc = np.arange(ATTN_WIN)[None, :] // CHUNK
    tbl = jnp.where((kc >= qc) & (kc <= qc + LEFT_CHUNKS), tbl, -jnp.inf)
    tbl = tbl.reshape(ATTN_HEADS // 2, 2, ATTN_TQ, ATTN_WIN).transpose(0, 3, 1, 2)
    tbl = tbl.reshape(ATTN_HEADS // 2, ATTN_WIN // ATTN_TQ, ATTN_TQ, 2 * ATTN_TQ)
    return tbl[:, np.asarray(BIAS_BLOCKS)]


def kernel(x, ffn1_norm_g, ffn1_w_gate, ffn1_w_up, ffn1_w_down, mix_norm_g, w_in,
           attn_q_norm_g, attn_k_norm_g, attn_rel_bias, hgrn_lower_bounds, hgrn_out_norm_g,
           w_out, ffn2_norm_g, ffn2_w_gate, ffn2_w_up, ffn2_w_down):
    bsz, seq, _ = x.shape
    depth = ffn1_norm_g.shape[0]
    assert depth == 1 and seq % MIX_ROWS == 0 and (bsz * seq) % ROW_TILE == 0
    head_of_col = np.arange(MXU_TILE) // ATTN_HEAD_DIM
    seg = jnp.asarray(head_of_col[:, None] == head_of_col[None, :], dtype=BF16)
    lsum = jnp.asarray(_level_sum_matrix(), dtype=BF16)
    lmask = jnp.asarray(_level_masks())
    row = lambda g: g.reshape(1, -1).astype(F32)
    tile_heads = lambda g: jnp.tile(g.astype(F32), ATTN_HEADS).reshape(1, ATTN_WIDTH)

    xf = x.reshape(bsz * seq, D_MODEL)
    for l in range(depth):
        later = (w_in[l], w_out[l], ffn2_w_gate[l], ffn2_w_up[l], ffn2_w_down[l])
        x1, w_in_b, w_out_b, wg2_b, wu2_b, wd2_b = _ffn_call(
            xf, row(ffn1_norm_g[l]), ffn1_w_gate[l].astype(BF16), ffn1_w_up[l].astype(BF16),
            ffn1_w_down[l].astype(BF16), cast=tuple(w.astype(F32) for w in later))
        mixed = _mixer_call(
            x1.reshape(bsz, seq, D_MODEL), row(mix_norm_g[l]), w_in_b,
            tile_heads(attn_q_norm_g[l]), tile_heads(attn_k_norm_g[l]), hgrn_lower_bounds.astype(F32),
            seg, _rel_bias_table(attn_rel_bias[l]), lsum, lmask, row(hgrn_out_norm_g[l]))
        xf, = _ffn_call(x1, row(ffn2_norm_g[l]), wg2_b, wu2_b, wd2_b,
                        mix=(mixed.reshape(bsz * seq, D_MODEL), w_out_b))
    return xf.reshape(bsz, seq, D_MODEL)
```

```python
import functools

import numpy as np
import jax
import jax.numpy as jnp
from jax import lax
from jax.experimental import pallas as pl
from jax.experimental.pallas import tpu as pltpu

D_MODEL = 1024
CHUNK = 64
ATTN_WIDTH = 512
HGRN_WIDTH = 512
ATTN_HEAD_DIM = 64
ATTN_HEADS = 8
HGRN_HEAD_DIM = 128
HGRN_HEADS = 4
LEFT_CHUNKS = 8
PAD = LEFT_CHUNKS * CHUNK
REL_CLIP = 128
D_FF = 2816
RMS_EPS = 1e-6
PROJ_GROUPS = 7
PROJ_COLS = PROJ_GROUPS * ATTN_WIDTH

LANES = 128
SUBLANES = 8
BF16_ROWS = 2 * SUBLANES
MXU_TILE = 256
FF_TILE = MXU_TILE
ROW_TILE = 1024
FFN_ROW_PIECES = 4
LOG2E = float(np.log2(np.e))
ATTN_Q_CHUNKS = 2
ATTN_TQ = ATTN_Q_CHUNKS * CHUNK
ATTN_WIN = PAD + ATTN_TQ
MIX_ROWS = 512
N_LEVELS = 6
LEVELS = tuple(CHUNK >> (lv + 1) for lv in range(N_LEVELS))
FINE_LEVELS = tuple(c for c in LEVELS if c < SUBLANES)
VMEM_LIMIT = 56 * 1024 * 1024

BF16 = jnp.bfloat16
F32 = jnp.float32


def _dot(a, b):
    return jnp.dot(a, b, preferred_element_type=F32)


def _dot_nt(a, b):
    return lax.dot_general(a, b, (((1,), (1,)), ((), ())), preferred_element_type=F32)


def _dot_tn(a, b):
    return lax.dot_general(a, b, (((0,), (0,)), ((), ())), preferred_element_type=F32)


def _silu(x):
    return x / (1.0 + jnp.exp(-x))


def _rms(x, g):
    ms = jnp.mean(x * x, axis=-1, keepdims=True)
    return x * lax.rsqrt(ms + RMS_EPS) * g


def _resident(shape):
    return pl.BlockSpec(shape, lambda *_: (0,) * len(shape), pipeline_mode=pl.Buffered(1))


ROW_PIECES = [slice(r * ROW_TILE // FFN_ROW_PIECES, (r + 1) * ROW_TILE // FFN_ROW_PIECES)
              for r in range(FFN_ROW_PIECES)]


def _ffn_body(xs, g_ref, wg_ref, wu_ref, wd_ref, o_ref, act_ref):
    first = slice(0, FF_TILE)
    hs, gates, ups = [], [], []
    for xr in xs:
        hs.append(_rms(xr, g_ref[...]).astype(BF16))
        gates.append(_dot(hs[-1], wg_ref[:, first]))
        ups.append(_dot(hs[-1], wu_ref[:, first]))
    gate, up = jnp.concatenate(gates, axis=0), jnp.concatenate(ups, axis=0)
    act_ref[:, first] = (_silu(gate) * up).astype(BF16)
    h = jnp.concatenate(hs, axis=0)
    for j in range(1, D_FF // FF_TILE):
        cols = slice(j * FF_TILE, (j + 1) * FF_TILE)
        gate = _dot(h, wg_ref[:, cols])
        up = _dot(h, wu_ref[:, cols])
        act_ref[:, cols] = (_silu(gate) * up).astype(BF16)
    y = _dot(act_ref[...], wd_ref[...])
    o_ref[...] = jnp.concatenate(xs, axis=0) + 0.5 * y


def _ffn_kernel(n_cast, x_ref, g_ref, wg_ref, wu_ref, wd_ref, *rest):
    cast_in, (o_ref, *cast_out, act_ref) = rest[:n_cast], rest[n_cast:]
    for src, dst in zip(cast_in, cast_out):
        dst[...] = src[...].astype(BF16)
    _ffn_body([x_ref[rows, :] for rows in ROW_PIECES], g_ref, wg_ref, wu_ref, wd_ref, o_ref, act_ref)


def _mix_ffn_kernel(x_ref, m_ref, wo_ref, g_ref, wg_ref, wu_ref, wd_ref, o_ref, act_ref):
    xs = [x_ref[rows, :] + _dot(m_ref[rows, :], wo_ref[...]) for rows in ROW_PIECES]
    _ffn_body(xs, g_ref, wg_ref, wu_ref, wd_ref, o_ref, act_ref)


def _rows(width):
    return pl.BlockSpec((ROW_TILE, width), lambda i: (i, 0))


def _slab_spec(shape, steps):
    rows, width = shape
    per = 1 if (rows // steps) % BF16_ROWS == 0 and rows % steps == 0 else 2
    slab = rows * per // steps
    assert slab * steps == rows * per and slab % BF16_ROWS == 0
    return pl.BlockSpec((slab, width), lambda i: (i // per, 0))


def _ffn_call(x, g, wg, wu, wd, mix=None, cast=()):
    m = x.shape[0]
    steps = m // ROW_TILE
    w_specs = [_resident((1, D_MODEL)), _resident((D_MODEL, D_FF)),
               _resident((D_MODEL, D_FF)), _resident((D_FF, D_MODEL))]
    out_specs, out_shape = [_rows(D_MODEL)], [jax.ShapeDtypeStruct((m, D_MODEL), F32)]
    if mix is None:
        kern = functools.partial(_ffn_kernel, len(cast))
        ins = (x, g, wg, wu, wd) + tuple(cast)
        specs = [_rows(D_MODEL)] + w_specs + [_slab_spec(w.shape, steps) for w in cast]
        out_specs += [_slab_spec(w.shape, steps) for w in cast]
        out_shape += [jax.ShapeDtypeStruct(w.shape, BF16) for w in cast]
    else:
        assert not cast
        mixed, wo = mix
        kern = _mix_ffn_kernel
        ins = (x, mixed, wo, g, wg, wu, wd)
        specs = [_rows(D_MODEL), _rows(D_MODEL), _resident((D_MODEL, D_MODEL))] + w_specs
    return pl.pallas_call(
        kern,
        grid=(steps,),
        in_specs=specs,
        out_specs=out_specs,
        out_shape=out_shape,
        scratch_shapes=[pltpu.VMEM((ROW_TILE, D_FF), BF16)],
        compiler_params=pltpu.CompilerParams(
            dimension_semantics=("arbitrary",), vmem_limit_bytes=VMEM_LIMIT),
        name="mix_ffn" if mix is not None else "ffn",
    )(*ins)


def _projection_stages(x_ref, g_ref, w_ref, qg_ref, kg_ref, lbp_ref, seg_ref, h_ref, dst):
    names = ("q", "k", "v", "rq", "forget", "rv", "sg")
    per_group = ATTN_WIDTH // MXU_TILE

    def normalize():
        h_ref[...] = _rms(x_ref[0], g_ref[...]).astype(BF16)

    def project(i):
        return _dot(h_ref[...], w_ref[:, i * MXU_TILE:(i + 1) * MXU_TILE])

    def head_normed(a, gain):
        ms = _dot((a * a).astype(BF16), seg_ref[...]) * (1.0 / ATTN_HEAD_DIM)
        return (a * lax.rsqrt(ms + RMS_EPS) * gain).astype(BF16)

    def tail(i, val):
        group, part = divmod(i, per_group)
        cols = slice(part * MXU_TILE, (part + 1) * MXU_TILE)
        name = names[group]
        if name == "q":
            dst["q"](cols, head_normed(val, qg_ref[:, cols] * (ATTN_HEAD_DIM ** -0.5 * LOG2E)))
        elif name == "k":
            dst["k"](cols, head_normed(val, kg_ref[:, cols]))
        elif name in ("v", "rv"):
            dst[name](cols, val.astype(BF16))
        elif name in ("rq", "sg"):
            dst[name](cols, _silu(val).astype(BF16))
        else:
            lbp = lbp_ref[:, cols]
            e = jnp.exp(lbp - jnp.max(lbp, axis=0, keepdims=True))
            lb = e[0:1, :] / jnp.sum(e, axis=0, keepdims=True)
            en = jnp.exp(-jnp.abs(val))
            big, small = 1.0 / (1.0 + en), en / (1.0 + en)
            pos = val >= 0
            f = lb + (1.0 - lb) * jnp.where(pos, big, small)
            dst["lf"](cols, jnp.log2(f).astype(BF16))
            dst["rk"](cols, ((1.0 - lb) * jnp.where(pos, small, big)).astype(BF16))

    return PROJ_GROUPS * per_group, normalize, (project, tail)


def _attention_stages(tile, src, bias_ref, o_ref):
    n_pairs = ATTN_HEADS // 2
    groups = MIX_ROWS // ATTN_TQ

    def where(u):
        g, p = divmod(u, n_pairs)
        group = tile * groups + g
        window = pl.ds(pl.multiple_of(group * ATTN_TQ, ATTN_TQ), ATTN_WIN)
        return group, window, slice(g * ATTN_TQ, (g + 1) * ATTN_TQ), slice(p * LANES, (p + 1) * LANES), p

    def scores(u):
        group, window, rows, cols, p = where(u)
        qt = src["qt"](cols, rows)
        zero = jnp.zeros((ATTN_HEAD_DIM, ATTN_TQ), BF16)
        q2 = jnp.concatenate([jnp.concatenate([qt[:ATTN_HEAD_DIM], zero], axis=0),
                              jnp.concatenate([zero, qt[ATTN_HEAD_DIM:]], axis=0)], axis=1)
        st = _dot(src["k"](window, cols), q2)
        first_valid = PAD // ATTN_TQ - group
        blocks = []
        for j in range(ATTN_WIN // ATTN_TQ):
            sj = st[j * ATTN_TQ:(j + 1) * ATTN_TQ]
            if j in BIAS_BLOCKS:
                sj = sj + bias_ref[p, BIAS_BLOCKS.index(j)]
            blocks.append(jnp.where(j >= first_valid, sj, -jnp.inf))
        return jnp.concatenate(blocks, axis=0)

    def weigh(u, st):
        _, window, rows, cols, _ = where(u)
        e = jnp.exp2(st - jnp.max(st, axis=0, keepdims=True)).astype(BF16)
        vt1 = jnp.concatenate([src["vt"](cols, window), src["ones"](window)], axis=0)
        ot = _dot(vt1, e)
        ot = ot[:LANES] / ot[LANES:LANES + 1]
        same_head = jnp.concatenate([ot[:ATTN_HEAD_DIM, :ATTN_TQ], ot[ATTN_HEAD_DIM:, ATTN_TQ:]], axis=0)
        o_ref[0, rows, cols] = same_head.astype(BF16).T

    return groups * n_pairs, (scores, weigh)


def _level_sum_matrix():
    u = np.arange(CHUNK)[:, None]
    r = np.arange(CHUNK)[None, :]
    mats = [(r <= u)]
    for c in FINE_LEVELS:
        m = (u // (2 * c)) * (2 * c) + c
        upper = (u & c) != 0
        mats.append(np.where(upper, (r > m) & (r <= u), (r > u) & (r <= m)))
    return np.concatenate(mats, axis=0).astype(np.float32)


def _level_masks():
    t = np.arange(CHUNK)[:, None]
    s = np.arange(CHUNK)[None, :]
    return np.stack([t == s] + [(t ^ s) < 2 * c for c in LEVELS]).astype(np.float32)


def _hgrn_stages(src, lsum_ref, lmask_ref, og_ref, o_ref, state_ref):
    t_row = lax.broadcasted_iota(jnp.int32, (CHUNK, HGRN_HEAD_DIM), 0)
    esums = {}

    def where(n):
        u, hd = divmod(n, HGRN_HEADS)
        return u, slice(u * CHUNK, (u + 1) * CHUNK), slice(hd * HGRN_HEAD_DIM, (hd + 1) * HGRN_HEAD_DIM), hd

    def coarse_operands(q, k, b, c):
        qs, ks = [], []
        zero = jnp.zeros((c, HGRN_HEAD_DIM), F32)
        for blk in range(CHUNK // c):
            r = slice(blk * c, (blk + 1) * c)
            m = (blk // 2) * 2 * c + c
            b_m = b[m:m + 1, :]
            qs.append(q[r] * jnp.exp2(b[r] - b_m) if blk % 2 else zero)
            ks.append(zero if blk % 2 else k[r] * jnp.exp2(b_m - b[r]))
        return jnp.concatenate(qs, axis=0).astype(BF16), jnp.concatenate(ks, axis=0).astype(BF16)

    def fine_operands(q, k, w, c):
        upper = (t_row & c) != 0
        return (jnp.where(upper, q * w, 0.0).astype(BF16), jnp.where(upper, 0.0, k * w).astype(BF16))

    def products(n):
        u, rows, cols, _ = where(n)
        if u not in esums:
            esums[u] = _dot(lsum_ref[...], src["lf"](rows, slice(None)))
        esum = esums[u]
        q = src["rq"](rows, cols).astype(F32)
        k = src["rk"](rows, cols).astype(F32)
        b = esum[:CHUNK, cols]
        b_last = b[CHUNK - 1:CHUNK, :]
        prods = []
        for c in LEVELS:
            if c in FINE_LEVELS:
                i = 1 + FINE_LEVELS.index(c)
                qw, kw = fine_operands(q, k, jnp.exp2(esum[i * CHUNK:(i + 1) * CHUNK, cols]), c)
            else:
                qw, kw = coarse_operands(q, k, b, c)
            prods.append(_dot_nt(qw, kw))
        return dict(q_in=(q * jnp.exp2(b)).astype(BF16), k_up=(k * jnp.exp2(b_last - b)).astype(BF16),
                    decay=jnp.exp2(b_last), diag=jnp.sum(q * k, axis=-1, keepdims=True), prods=prods)

    def finish(n, a):
        _, rows, cols, hd = where(n)
        scores = a["diag"] * lmask_ref[0]
        for lv in range(N_LEVELS):
            scores = scores + a["prods"][lv] * lmask_ref[1 + lv]
        v = src["rv"](rows, cols)
        state = state_ref[hd]
        o = _dot_nt(a["q_in"], state.astype(BF16)) + _dot(scores.astype(BF16), v)
        state_ref[hd] = state * a["decay"] + _dot_tn(v, a["k_up"])
        y = _rms(o, og_ref[...]) * src["sg"](rows, cols).astype(F32)
        out_cols = slice(ATTN_WIDTH + hd * HGRN_HEAD_DIM, ATTN_WIDTH + (hd + 1) * HGRN_HEAD_DIM)
        o_ref[0, rows, out_cols] = y.astype(BF16)

    return (MIX_ROWS // CHUNK) * HGRN_HEADS, (products, finish)


def _mixer_kernel(x0_ref, xn_ref, g_ref, w_ref, qg_ref, kg_ref, lbp_ref, seg_ref, bias_ref, lsum_ref,
                  lmask_ref, og_ref, o_ref, h_ref, qt_ref, rq_ref, rk_ref, rv_ref, sg_ref, lf_ref,
                  kpad_ref, vt_ref, ones_ref, state_ref):
    seq = kpad_ref.shape[1] - PAD
    tiles = seq // MIX_ROWS
    s = pl.program_id(0)
    tile = lax.rem(s, tiles)
    nxt = jnp.minimum(s + 1, pl.num_programs(0) - 1)

    def projection(x_ref, step):
        slot = lax.rem(step, 2)
        par = lax.rem(lax.div(step, tiles), 2)
        rows = pl.ds(pl.multiple_of(PAD + lax.rem(step, tiles) * MIX_ROWS, MIX_ROWS), MIX_ROWS)

        def put(ref):
            def store(cols, val):
                ref[slot, :, cols] = val
            return store

        def put_k(cols, val):
            kpad_ref[par, rows, cols] = val

        def put_v(cols, val):
            vt_ref[par, cols, rows] = val.T

        def put_q(cols, val):
            qt_ref[slot, cols, :] = val.T

        dst = dict(q=put_q, k=put_k, v=put_v, rq=put(rq_ref), rk=put(rk_ref), rv=put(rv_ref),
                   sg=put(sg_ref), lf=put(lf_ref))
        return _projection_stages(x_ref, g_ref, w_ref, qg_ref, kg_ref, lbp_ref, seg_ref, h_ref, dst)

    @pl.when(s == 0)
    def _():
        kpad_ref[:, :PAD, :] = jnp.zeros((2, PAD, ATTN_WIDTH), BF16)
        vt_ref[:, :, :PAD] = jnp.zeros((2, ATTN_WIDTH, PAD), BF16)
        ones_ref[:, :PAD] = jnp.zeros((BF16_ROWS, PAD), BF16)
        ones_ref[:, PAD:] = jnp.ones((BF16_ROWS, seq), BF16)
        n_units, normalize, (project, tail) = projection(x0_ref, s)
        normalize()
        nxt_val = project(0)
        for i in range(n_units):
            cur, nxt_val = nxt_val, (project(i + 1) if i + 1 < n_units else None)
            tail(i, cur)

    @pl.when(tile == 0)
    def _():
        state_ref[...] = jnp.zeros_like(state_ref)

    slot = lax.rem(s, 2)
    par = lax.rem(lax.div(s, tiles), 2)
    tile_src = lambda ref: (lambda rows, cols: ref[slot, rows, cols])
    src = dict(qt=tile_src(qt_ref), rq=tile_src(rq_ref), rk=tile_src(rk_ref), rv=tile_src(rv_ref),
               sg=tile_src(sg_ref), lf=tile_src(lf_ref),
               k=lambda window, cols: kpad_ref[par, window, cols],
               vt=lambda cols, window: vt_ref[par, cols, window],
               ones=lambda window: ones_ref[:, window])
    n_proj, normalize, (proj_first, proj_second) = projection(xn_ref, nxt)
    n_attn, (attn_first, attn_second) = _attention_stages(tile, src, bias_ref, o_ref)
    n_hgrn, (hgrn_first, hgrn_second) = _hgrn_stages(src, lsum_ref, lmask_ref, og_ref, o_ref, state_ref)

    assert n_proj <= n_attn and n_hgrn % n_attn == 0
    per = n_hgrn // n_attn
    normalize()
    p_next, a_next, h_next = proj_first(0), attn_first(0), hgrn_first(0)
    for i in range(n_attn):
        a_cur, a_next = a_next, (attn_first(i + 1) if i + 1 < n_attn else None)
        for n in range(i * per, (i + 1) * per):
            h_cur, h_next = h_next, (hgrn_first(n + 1) if n + 1 < n_hgrn else None)
            hgrn_second(n, h_cur)
            if n == i * per:
                attn_second(i, a_cur)
                if i < n_proj:
                    p_cur, p_next = p_next, (proj_first(i + 1) if i + 1 < n_proj else None)
                    proj_second(i, p_cur)


def _mixer_call(x1, g, w_in, qg, kg, lbp, seg, bias, lsum, lmask, og):
    b, s, _ = x1.shape
    tiles = s // MIX_ROWS
    steps = b * tiles
    slot_pair = pltpu.VMEM((2, MIX_ROWS, HGRN_WIDTH), BF16)

    def next_tile(i):
        n = jnp.minimum(i + 1, steps - 1)
        return (n // tiles, n % tiles, 0)

    return pl.pallas_call(
        _mixer_kernel,
        grid=(steps,),
        in_specs=[pl.BlockSpec((1, MIX_ROWS, D_MODEL), lambda i: (0, 0, 0), pipeline_mode=pl.Buffered(1)),
                  pl.BlockSpec((1, MIX_ROWS, D_MODEL), next_tile),
                  _resident((1, D_MODEL)), _resident((D_MODEL, PROJ_COLS)),
                  _resident((1, ATTN_WIDTH)), _resident((1, ATTN_WIDTH)), _resident(lbp.shape),
                  _resident((MXU_TILE, MXU_TILE)), _resident(bias.shape), _resident(lsum.shape),
                  _resident(lmask.shape), _resident((1, HGRN_HEAD_DIM))],
        out_specs=pl.BlockSpec((1, MIX_ROWS, D_MODEL), lambda i: (i // tiles, i % tiles, 0)),
        out_shape=jax.ShapeDtypeStruct((b, s, D_MODEL), BF16),
        scratch_shapes=[pltpu.VMEM((MIX_ROWS, D_MODEL), BF16),
                        pltpu.VMEM((2, ATTN_WIDTH, MIX_ROWS), BF16)] + [slot_pair] * 5 + [
                        pltpu.VMEM((2, s + PAD, ATTN_WIDTH), BF16),
                        pltpu.VMEM((2, ATTN_WIDTH, s + PAD), BF16),
                        pltpu.VMEM((BF16_ROWS, s + PAD), BF16),
                        pltpu.VMEM((HGRN_HEADS, HGRN_HEAD_DIM, HGRN_HEAD_DIM), F32)],
        compiler_params=pltpu.CompilerParams(
            dimension_semantics=("arbitrary",), vmem_limit_bytes=VMEM_LIMIT),
        name="mixer",
    )(x1, x1, g, w_in, qg, kg, lbp, seg, bias, lsum, lmask, og)


def _bias_blocks():
    qq = np.arange(ATTN_TQ)[:, None]
    key = np.arange(ATTN_WIN)[None, :]
    near = PAD + qq - key < REL_CLIP
    band = (key // CHUNK >= qq // CHUNK) & (key // CHUNK <= qq // CHUNK + LEFT_CHUNKS)
    used = (near | ~band).any(axis=0).reshape(ATTN_WIN // ATTN_TQ, ATTN_TQ).any(axis=1)
    return tuple(int(j) for j in np.nonzero(used)[0])


BIAS_BLOCKS = _bias_blocks()


def _rel_bias_table(rel_bias):
    assert ATTN_TQ - 1 <= REL_CLIP
    rb = (rel_bias.astype(F32) - rel_bias.astype(F32)[:, 2 * REL_CLIP:]) * LOG2E
    n = ATTN_WIN + ATTN_TQ
    far = jnp.broadcast_to(rb[:, 2 * REL_CLIP:], (ATTN_HEADS, ATTN_WIN - REL_CLIP))
    near = rb[:, REL_CLIP - ATTN_TQ + 1:2 * REL_CLIP][:, ::-1]
    x = jnp.concatenate([far, near, jnp.zeros((ATTN_HEADS, 1), F32)], axis=1)
    rolled = jnp.tile(x, (1, ATTN_TQ))[:, :ATTN_TQ * (n - 1)].reshape(ATTN_HEADS, ATTN_TQ, n - 1)
    tbl = rolled[:, :, ATTN_TQ - 1:ATTN_TQ - 1 + ATTN_WIN]
    qc = np.arange(ATTN_TQ)[:, None] // CHUNK
    kc = np.arange(ATTN_WIN)[None, :] // CHUNK
    tbl = jnp.where((kc >= qc) & (kc <= qc + LEFT_CHUNKS), tbl, -jnp.inf)
    tbl = tbl.reshape(ATTN_HEADS // 2, 2, ATTN_TQ, ATTN_WIN).transpose(0, 3, 1, 2)
    tbl = tbl.reshape(ATTN_HEADS // 2, ATTN_WIN // ATTN_TQ, ATTN_TQ, 2 * ATTN_TQ)
    return tbl[:, np.asarray(BIAS_BLOCKS)]


def kernel(x, ffn1_norm_g, ffn1_w_gate, ffn1_w_up, ffn1_w_down, mix_norm_g, w_in,
           attn_q_norm_g, attn_k_norm_g, attn_rel_bias, hgrn_lower_bounds, hgrn_out_norm_g,
           w_out, ffn2_norm_g, ffn2_w_gate, ffn2_w_up, ffn2_w_down):
    bsz, seq, _ = x.shape
    depth = ffn1_norm_g.shape[0]
    assert depth == 1 and seq % MIX_ROWS == 0 and (bsz * seq) % ROW_TILE == 0
    head_of_col = np.arange(MXU_TILE) // ATTN_HEAD_DIM
    seg = jnp.asarray(head_of_col[:, None] == head_of_col[None, :], dtype=BF16)
    lsum = jnp.asarray(_level_sum_matrix(), dtype=BF16)
    lmask = jnp.asarray(_level_masks())
    row = lambda g: g.reshape(1, -1).astype(F32)
    tile_heads = lambda g: jnp.tile(g.astype(F32), ATTN_HEADS).reshape(1, ATTN_WIDTH)

    xf = x.reshape(bsz * seq, D_MODEL)
    for l in range(depth):
        later = (w_in[l], w_out[l], ffn2_w_gate[l], ffn2_w_up[l], ffn2_w_down[l])
        x1, w_in_b, w_out_b, wg2_b, wu2_b, wd2_b = _ffn_call(
            xf, row(ffn1_norm_g[l]), ffn1_w_gate[l].astype(BF16), ffn1_w_up[l].astype(BF16),
            ffn1_w_down[l].astype(BF16), cast=tuple(w.astype(F32) for w in later))
        mixed = _mixer_call(
            x1.reshape(bsz, seq, D_MODEL), row(mix_norm_g[l]), w_in_b,
            tile_heads(attn_q_norm_g[l]), tile_heads(attn_k_norm_g[l]), hgrn_lower_bounds.astype(F32),
            seg, _rel_bias_table(attn_rel_bias[l]), lsum, lmask, row(hgrn_out_norm_g[l]))
        xf, = _ffn_call(x1, row(ffn2_norm_g[l]), wg2_b, wu2_b, wd2_b,
                        mix=(mixed.reshape(bsz * seq, D_MODEL), w_out_b))
    return xf.reshape(bsz, seq, D_MODEL)
```

```python
import functools

import numpy as np
import jax
import jax.numpy as jnp
from jax import lax
from jax.experimental import pallas as pl
from jax.experimental.pallas import tpu as pltpu

D_MODEL = 1024
CHUNK = 64
ATTN_WIDTH = 512
HGRN_WIDTH = 512
ATTN_HEAD_DIM = 64
ATTN_HEADS = 8
HGRN_HEAD_DIM = 128
HGRN_HEADS = 4
LEFT_CHUNKS = 8
PAD = LEFT_CHUNKS * CHUNK
REL_CLIP = 128
D_FF = 2816
RMS_EPS = 1e-6
PROJ_GROUPS = 7
PROJ_COLS = PROJ_GROUPS * ATTN_WIDTH

LANES = 128
SUBLANES = 8
BF16_ROWS = 2 * SUBLANES
MXU_TILE = 256
FF_TILE = MXU_TILE
ROW_TILE = 1024
FFN_ROW_PIECES = 4
WEIGHT_SLABS = 8
LOG2E = float(np.log2(np.e))
ATTN_Q_CHUNKS = 2
ATTN_TQ = ATTN_Q_CHUNKS * CHUNK
ATTN_WIN = PAD + ATTN_TQ
MIX_ROWS = 512
N_LEVELS = 6
LEVELS = tuple(CHUNK >> (lv + 1) for lv in range(N_LEVELS))
FINE_LEVELS = tuple(c for c in LEVELS if c < SUBLANES)
VMEM_LIMIT = 56 * 1024 * 1024

BF16 = jnp.bfloat16
F32 = jnp.float32


def _dot(a, b):
    return jnp.dot(a, b, preferred_element_type=F32)


def _dot_nt(a, b):
    return lax.dot_general(a, b, (((1,), (1,)), ((), ())), preferred_element_type=F32)


def _dot_tn(a, b):
    return lax.dot_general(a, b, (((0,), (0,)), ((), ())), preferred_element_type=F32)


def _silu(x):
    return x / (1.0 + jnp.exp(-x))


def _rms(x, g):
    ms = jnp.mean(x * x, axis=-1, keepdims=True)
    return x * lax.rsqrt(ms + RMS_EPS) * g


def _resident(shape):
    return pl.BlockSpec(shape, lambda *_: (0,) * len(shape), pipeline_mode=pl.Buffered(1))


ROW_PIECES = [slice(r * ROW_TILE // FFN_ROW_PIECES, (r + 1) * ROW_TILE // FFN_ROW_PIECES)
              for r in range(FFN_ROW_PIECES)]


def _ffn_body(xs, g_ref, wg_ref, wu_ref, wd_ref, o_ref, act_ref):
    first = slice(0, FF_TILE)
    hs, gates, ups = [], [], []
    for xr in xs:
        hs.append(_rms(xr, g_ref[...]).astype(BF16))
        gates.append(_dot(hs[-1], wg_ref[:, first]))
        ups.append(_dot(hs[-1], wu_ref[:, first]))
    gate, up = jnp.concatenate(gates, axis=0), jnp.concatenate(ups, axis=0)
    act_ref[:, first] = (_silu(gate) * up).astype(BF16)
    h = jnp.concatenate(hs, axis=0)
    for j in range(1, D_FF // FF_TILE):
        cols = slice(j * FF_TILE, (j + 1) * FF_TILE)
        gate = _dot(h, wg_ref[:, cols])
        up = _dot(h, wu_ref[:, cols])
        act_ref[:, cols] = (_silu(gate) * up).astype(BF16)
    y = _dot(act_ref[...], wd_ref[...])
    o_ref[...] = jnp.concatenate(xs, axis=0) + 0.5 * y


def _load_weights_bf16(hbm_refs, vmem_refs, stage_refs, sem_ref):
    jobs = []
    for w, (src, dst) in enumerate(zip(hbm_refs, vmem_refs)):
        stage = stage_refs[0] if src.shape[1] == stage_refs[0].shape[2] else stage_refs[1]
        rows = stage.shape[1]
        assert src.shape[0] % rows == 0 and src.shape[1] == stage.shape[2]
        jobs += [(src, dst, stage, r0, rows) for r0 in range(0, src.shape[0], rows)]

    def copy(n):
        src, _, stage, r0, rows = jobs[n]
        slot = n % 2
        which = 0 if stage is stage_refs[0] else 1
        return pltpu.make_async_copy(src.at[pl.ds(r0, rows), :], stage.at[slot], sem_ref.at[which, slot])

    copy(0).start()
    for n, (_, dst, stage, r0, rows) in enumerate(jobs):
        if n + 1 < len(jobs):
            copy(n + 1).start()
        copy(n).wait()
        dst[pl.ds(r0, rows), :] = stage[n % 2].astype(BF16)


def _ffn_kernel(n_cast, x_ref, g_ref, wg_hbm, wu_hbm, wd_hbm, *rest):
    cast_in, rest = rest[:n_cast], rest[n_cast:]
    o_ref, cast_out = rest[0], rest[1:1 + n_cast]
    act_ref, wg_ref, wu_ref, wd_ref, stage_up, stage_down, sem_ref = rest[1 + n_cast:]

    @pl.when(pl.program_id(0) == 0)
    def _():
        _load_weights_bf16((wg_hbm, wu_hbm, wd_hbm), (wg_ref, wu_ref, wd_ref),
                           (stage_up, stage_down), sem_ref)

    for src, dst in zip(cast_in, cast_out):
        dst[...] = src[...].astype(BF16)
    _ffn_body([x_ref[rows, :] for rows in ROW_PIECES], g_ref, wg_ref, wu_ref, wd_ref, o_ref, act_ref)


def _mix_ffn_kernel(x_ref, m_ref, wo_ref, g_ref, wg_ref, wu_ref, wd_ref, o_ref, act_ref):
    xs = [x_ref[rows, :] + _dot(m_ref[rows, :], wo_ref[...]) for rows in ROW_PIECES]
    _ffn_body(xs, g_ref, wg_ref, wu_ref, wd_ref, o_ref, act_ref)


def _rows(width):
    return pl.BlockSpec((ROW_TILE, width), lambda i: (i, 0))


def _slab_spec(shape, steps):
    rows, width = shape
    per = 1 if (rows // steps) % BF16_ROWS == 0 and rows % steps == 0 else 2
    slab = rows * per // steps
    assert slab * steps == rows * per and slab % BF16_ROWS == 0
    return pl.BlockSpec((slab, width), lambda i: (i // per, 0))


def _ffn_call(x, g, wg, wu, wd, mix=None, cast=()):
    m = x.shape[0]
    steps = m // ROW_TILE
    w_specs = [_resident((1, D_MODEL)), _resident((D_MODEL, D_FF)),
               _resident((D_MODEL, D_FF)), _resident((D_FF, D_MODEL))]
    out_specs, out_shape = [_rows(D_MODEL)], [jax.ShapeDtypeStruct((m, D_MODEL), F32)]
    scratch = [pltpu.VMEM((ROW_TILE, D_FF), BF16)]
    if mix is None:
        kern = functools.partial(_ffn_kernel, len(cast))
        ins = (x, g, wg, wu, wd) + tuple(cast)
        in_hbm = pl.BlockSpec(memory_space=pl.ANY)
        specs = ([_rows(D_MODEL), w_specs[0], in_hbm, in_hbm, in_hbm]
                 + [_slab_spec(w.shape, steps) for w in cast])
        out_specs += [_slab_spec(w.shape, steps) for w in cast]
        out_shape += [jax.ShapeDtypeStruct(w.shape, BF16) for w in cast]
        scratch += [pltpu.VMEM(wg.shape, BF16), pltpu.VMEM(wu.shape, BF16), pltpu.VMEM(wd.shape, BF16),
                    pltpu.VMEM((2, D_MODEL // WEIGHT_SLABS, D_FF), F32),
                    pltpu.VMEM((2, D_FF // WEIGHT_SLABS, D_MODEL), F32),
                    pltpu.SemaphoreType.DMA((2, 2))]
    else:
        assert not cast
        mixed, wo = mix
        kern = _mix_ffn_kernel
        ins = (x, mixed, wo, g, wg, wu, wd)
        specs = [_rows(D_MODEL), _rows(D_MODEL), _resident((D_MODEL, D_MODEL))] + w_specs
    return pl.pallas_call(
        kern,
        grid=(steps,),
        in_specs=specs,
        out_specs=out_specs,
        out_shape=out_shape,
        scratch_shapes=scratch,
        compiler_params=pltpu.CompilerParams(
            dimension_semantics=("arbitrary",), vmem_limit_bytes=VMEM_LIMIT),
        name="mix_ffn" if mix is not None else "ffn",
    )(*ins)


def _projection_stages(x_ref, g_ref, w_ref, qg_ref, kg_ref, lbp_ref, seg_ref, h_ref, dst):
    names = ("q", "k", "v", "rq", "forget", "rv", "sg")
    per_group = ATTN_WIDTH // MXU_TILE

    def normalize():
        h_ref[...] = _rms(x_ref[0], g_ref[...]).astype(BF16)

    def project(i):
        return _dot(h_ref[...], w_ref[:, i * MXU_TILE:(i + 1) * MXU_TILE])

    def head_normed(a, gain):
        ms = _dot((a * a).astype(BF16), seg_ref[...]) * (1.0 / ATTN_HEAD_DIM)
        return (a * lax.rsqrt(ms + RMS_EPS) * gain).astype(BF16)

    def tail(i, val):
        group, part = divmod(i, per_group)
        cols = slice(part * MXU_TILE, (part + 1) * MXU_TILE)
        name = names[group]
        if name == "q":
            dst["q"](cols, head_normed(val, qg_ref[:, cols] * (ATTN_HEAD_DIM ** -0.5 * LOG2E)))
        elif name == "k":
            dst["k"](cols, head_normed(val, kg_ref[:, cols]))
        elif name in ("v", "rv"):
            dst[name](cols, val.astype(BF16))
        elif name in ("rq", "sg"):
            dst[name](cols, _silu(val).astype(BF16))
        else:
            lbp = lbp_ref[:, cols]
            e = jnp.exp(lbp - jnp.max(lbp, axis=0, keepdims=True))
            lb = e[0:1, :] / jnp.sum(e, axis=0, keepdims=True)
            en = jnp.exp(-jnp.abs(val))
            big, small = 1.0 / (1.0 + en), en / (1.0 + en)
            pos = val >= 0
            f = lb + (1.0 - lb) * jnp.where(pos, big, small)
            dst["lf"](cols, jnp.log2(f).astype(BF16))
            dst["rk"](cols, ((1.0 - lb) * jnp.where(pos, small, big)).astype(BF16))

    return PROJ_GROUPS * per_group, normalize, (project, tail)


def _attention_stages(tile, src, bias_ref, o_ref):
    n_pairs = ATTN_HEADS // 2
    groups = MIX_ROWS // ATTN_TQ

    def where(u):
        g, p = divmod(u, n_pairs)
        group = tile * groups + g
        window = pl.ds(pl.multiple_of(group * ATTN_TQ, ATTN_TQ), ATTN_WIN)
        return group, window, slice(g * ATTN_TQ, (g + 1) * ATTN_TQ), slice(p * LANES, (p + 1) * LANES), p

    def scores(u):
        group, window, rows, cols, p = where(u)
        qt = src["qt"](cols, rows)
        zero = jnp.zeros((ATTN_HEAD_DIM, ATTN_TQ), BF16)
        q2 = jnp.concatenate([jnp.concatenate([qt[:ATTN_HEAD_DIM], zero], axis=0),
                              jnp.concatenate([zero, qt[ATTN_HEAD_DIM:]], axis=0)], axis=1)
        st = _dot(src["k"](window, cols), q2)
        first_valid = PAD // ATTN_TQ - group
        blocks = []
        for j in range(ATTN_WIN // ATTN_TQ):
            sj = st[j * ATTN_TQ:(j + 1) * ATTN_TQ]
            if j in BIAS_BLOCKS:
                sj = sj + bias_ref[p, BIAS_BLOCKS.index(j)]
            blocks.append(jnp.where(j >= first_valid, sj, -jnp.inf))
        return jnp.concatenate(blocks, axis=0)

    def weigh(u, st):
        _, window, rows, cols, _ = where(u)
        e = jnp.exp2(st - jnp.max(st, axis=0, keepdims=True)).astype(BF16)
        vt1 = jnp.concatenate([src["vt"](cols, window), src["ones"](window)], axis=0)
        ot = _dot(vt1, e)
        ot = ot[:LANES] / ot[LANES:LANES + 1]
        same_head = jnp.concatenate([ot[:ATTN_HEAD_DIM, :ATTN_TQ], ot[ATTN_HEAD_DIM:, ATTN_TQ:]], axis=0)
        o_ref[0, rows, cols] = same_head.astype(BF16).T

    return groups * n_pairs, (scores, weigh)


def _level_sum_matrix():
    u = np.arange(CHUNK)[:, None]
    r = np.arange(CHUNK)[None, :]
    mats = [(r <= u)]
    for c in FINE_LEVELS:
        m = (u // (2 * c)) * (2 * c) + c
        upper = (u & c) != 0
        mats.append(np.where(upper, (r > m) & (r <= u), (r > u) & (r <= m)))
    return np.concatenate(mats, axis=0).astype(np.float32)


def _level_masks():
    t = np.arange(CHUNK)[:, None]
    s = np.arange(CHUNK)[None, :]
    return np.stack([t == s] + [(t ^ s) < 2 * c for c in LEVELS]).astype(np.float32)


def _hgrn_stages(src, lsum_ref, lmask_ref, og_ref, o_ref, state_ref):
    t_row = lax.broadcasted_iota(jnp.int32, (CHUNK, HGRN_HEAD_DIM), 0)
    esums = {}

    def where(n):
        u, hd = divmod(n, HGRN_HEADS)
        return u, slice(u * CHUNK, (u + 1) * CHUNK), slice(hd * HGRN_HEAD_DIM, (hd + 1) * HGRN_HEAD_DIM), hd

    def coarse_operands(q, k, b, c):
        qs, ks = [], []
        zero = jnp.zeros((c, HGRN_HEAD_DIM), F32)
        for blk in range(CHUNK // c):
            r = slice(blk * c, (blk + 1) * c)
            m = (blk // 2) * 2 * c + c
            b_m = b[m:m + 1, :]
            qs.append(q[r] * jnp.exp2(b[r] - b_m) if blk % 2 else zero)
            ks.append(zero if blk % 2 else k[r] * jnp.exp2(b_m - b[r]))
        return jnp.concatenate(qs, axis=0).astype(BF16), jnp.concatenate(ks, axis=0).astype(BF16)

    def fine_operands(q, k, w, c):
        upper = (t_row & c) != 0
        return (jnp.where(upper, q * w, 0.0).astype(BF16), jnp.where(upper, 0.0, k * w).astype(BF16))

    def products(n):
        u, rows, cols, _ = where(n)
        if u not in esums:
            esums[u] = _dot(lsum_ref[...], src["lf"](rows, slice(None)))
        esum = esums[u]
        q = src["rq"](rows, cols).astype(F32)
        k = src["rk"](rows, cols).astype(F32)
        b = esum[:CHUNK, cols]
        b_last = b[CHUNK - 1:CHUNK, :]
        prods = []
        for c in LEVELS:
            if c in FINE_LEVELS:
                i = 1 + FINE_LEVELS.index(c)
                qw, kw = fine_operands(q, k, jnp.exp2(esum[i * CHUNK:(i + 1) * CHUNK, cols]), c)
            else:
                qw, kw = coarse_operands(q, k, b, c)
            prods.append(_dot_nt(qw, kw))
        return dict(q_in=(q * jnp.exp2(b)).astype(BF16), k_up=(k * jnp.exp2(b_last - b)).astype(BF16),
                    decay=jnp.exp2(b_last), diag=jnp.sum(q * k, axis=-1, keepdims=True), prods=prods)

    def finish(n, a):
        _, rows, cols, hd = where(n)
        scores = a["diag"] * lmask_ref[0]
        for lv in range(N_LEVELS):
            scores = scores + a["prods"][lv] * lmask_ref[1 + lv]
        v = src["rv"](rows, cols)
        state = state_ref[hd]
        o = _dot_nt(a["q_in"], state.astype(BF16)) + _dot(scores.astype(BF16), v)
        state_ref[hd] = state * a["decay"] + _dot_tn(v, a["k_up"])
        y = _rms(o, og_ref[...]) * src["sg"](rows, cols).astype(F32)
        out_cols = slice(ATTN_WIDTH + hd * HGRN_HEAD_DIM, ATTN_WIDTH + (hd + 1) * HGRN_HEAD_DIM)
        o_ref[0, rows, out_cols] = y.astype(BF16)

    return (MIX_ROWS // CHUNK) * HGRN_HEADS, (products, finish)


def _mixer_kernel(x0_ref, xn_ref, g_ref, w_ref, qg_ref, kg_ref, lbp_ref, seg_ref, bias_ref, lsum_ref,
                  lmask_ref, og_ref, o_ref, h_ref, qt_ref, rq_ref, rk_ref, rv_ref, sg_ref, lf_ref,
                  kpad_ref, vt_ref, ones_ref, state_ref):
    seq = kpad_ref.shape[1] - PAD
    tiles = seq // MIX_ROWS
    s = pl.program_id(0)
    tile = lax.rem(s, tiles)
    nxt = jnp.minimum(s + 1, pl.num_programs(0) - 1)

    def projection(x_ref, step):
        slot = lax.rem(step, 2)
        par = lax.rem(lax.div(step, tiles), 2)
        rows = pl.ds(pl.multiple_of(PAD + lax.rem(step, tiles) * MIX_ROWS, MIX_ROWS), MIX_ROWS)

        def put(ref):
            def store(cols, val):
                ref[slot, :, cols] = val
            return store

        def put_k(cols, val):
            kpad_ref[par, rows, cols] = val

        def put_v(cols, val):
            vt_ref[par, cols, rows] = val.T

        def put_q(cols, val):
            qt_ref[slot, cols, :] = val.T

        dst = dict(q=put_q, k=put_k, v=put_v, rq=put(rq_ref), rk=put(rk_ref), rv=put(rv_ref),
                   sg=put(sg_ref), lf=put(lf_ref))
        return _projection_stages(x_ref, g_ref, w_ref, qg_ref, kg_ref, lbp_ref, seg_ref, h_ref, dst)

    @pl.when(s == 0)
    def _():
        kpad_ref[:, :PAD, :] = jnp.zeros((2, PAD, ATTN_WIDTH), BF16)
        vt_ref[:, :, :PAD] = jnp.zeros((2, ATTN_WIDTH, PAD), BF16)
        ones_ref[:, :PAD] = jnp.zeros((BF16_ROWS, PAD), BF16)
        ones_ref[:, PAD:] = jnp.ones((BF16_ROWS, seq), BF16)
        n_units, normalize, (project, tail) = projection(x0_ref, s)
        normalize()
        nxt_val = project(0)
        for i in range(n_units):
            cur, nxt_val = nxt_val, (project(i + 1) if i + 1 < n_units else None)
            tail(i, cur)

    @pl.when(tile == 0)
    def _():
        state_ref[...] = jnp.zeros_like(state_ref)

    slot = lax.rem(s, 2)
    par = lax.rem(lax.div(s, tiles), 2)
    tile_src = lambda ref: (lambda rows, cols: ref[slot, rows, cols])
    src = dict(qt=tile_src(qt_ref), rq=tile_src(rq_ref), rk=tile_src(rk_ref), rv=tile_src(rv_ref),
               sg=tile_src(sg_ref), lf=tile_src(lf_ref),
               k=lambda window, cols: kpad_ref[par, window, cols],
               vt=lambda cols, window: vt_ref[par, cols, window],
               ones=lambda window: ones_ref[:, window])
    n_proj, normalize, (proj_first, proj_second) = projection(xn_ref, nxt)
    n_attn, (attn_first, attn_second) = _attention_stages(tile, src, bias_ref, o_ref)
    n_hgrn, (hgrn_first, hgrn_second) = _hgrn_stages(src, lsum_ref, lmask_ref, og_ref, o_ref, state_ref)

    assert n_proj <= n_attn and n_hgrn % n_attn == 0
    per = n_hgrn // n_attn
    normalize()
    p_next, a_next, h_next = proj_first(0), attn_first(0), hgrn_first(0)
    for i in range(n_attn):
        a_cur, a_next = a_next, (attn_first(i + 1) if i + 1 < n_attn else None)
        for n in range(i * per, (i + 1) * per):
            h_cur, h_next = h_next, (hgrn_first(n + 1) if n + 1 < n_hgrn else None)
            hgrn_second(n, h_cur)
            if n == i * per:
                attn_second(i, a_cur)
                if i < n_proj:
                    p_cur, p_next = p_next, (proj_first(i + 1) if i + 1 < n_proj else None)
                    proj_second(i, p_cur)


def _mixer_call(x1, g, w_in, qg, kg, lbp, seg, bias, lsum, lmask, og):
    b, s, _ = x1.shape
    tiles = s // MIX_ROWS
    steps = b * tiles
    slot_pair = pltpu.VMEM((2, MIX_ROWS, HGRN_WIDTH), BF16)

    def next_tile(i):
        n = jnp.minimum(i + 1, steps - 1)
        return (n // tiles, n % tiles, 0)

    return pl.pallas_call(
        _mixer_kernel,
        grid=(steps,),
        in_specs=[pl.BlockSpec((1, MIX_ROWS, D_MODEL), lambda i: (0, 0, 0), pipeline_mode=pl.Buffered(1)),
                  pl.BlockSpec((1, MIX_ROWS, D_MODEL), next_tile),
                  _resident((1, D_MODEL)), _resident((D_MODEL, PROJ_COLS)),
                  _resident((1, ATTN_WIDTH)), _resident((1, ATTN_WIDTH)), _resident(lbp.shape),
                  _resident((MXU_TILE, MXU_TILE)), _resident(bias.shape), _resident(lsum.shape),
                  _resident(lmask.shape), _resident((1, HGRN_HEAD_DIM))],
        out_specs=pl.BlockSpec((1, MIX_ROWS, D_MODEL), lambda i: (i // tiles, i % tiles, 0)),
        out_shape=jax.ShapeDtypeStruct((b, s, D_MODEL), BF16),
        scratch_shapes=[pltpu.VMEM((MIX_ROWS, D_MODEL), BF16),
                        pltpu.VMEM((2, ATTN_WIDTH, MIX_ROWS), BF16)] + [slot_pair] * 5 + [
                        pltpu.VMEM((2, s + PAD, ATTN_WIDTH), BF16),
                        pltpu.VMEM((2, ATTN_WIDTH, s + PAD), BF16),
                        pltpu.VMEM((BF16_ROWS, s + PAD), BF16),
                        pltpu.VMEM((HGRN_HEADS, HGRN_HEAD_DIM, HGRN_HEAD_DIM), F32)],
        compiler_params=pltpu.CompilerParams(
            dimension_semantics=("arbitrary",), vmem_limit_bytes=VMEM_LIMIT),
        name="mixer",
    )(x1, x1, g, w_in, qg, kg, lbp, seg, bias, lsum, lmask, og)


def _bias_blocks():
    qq = np.arange(ATTN_TQ)[:, None]
    key = np.arange(ATTN_WIN)[None, :]
    near = PAD + qq - key < REL_CLIP
    band = (key // CHUNK >= qq // CHUNK) & (key // CHUNK <= qq // CHUNK + LEFT_CHUNKS)
    used = (near | ~band).any(axis=0).reshape(ATTN_WIN // ATTN_TQ, ATTN_TQ).any(axis=1)
    return tuple(int(j) for j in np.nonzero(used)[0])


BIAS_BLOCKS = _bias_blocks()


def _rel_bias_table(rel_bias):
    assert ATTN_TQ - 1 <= REL_CLIP
    rb = (rel_bias.astype(F32) - rel_bias.astype(F32)[:, 2 * REL_CLIP:]) * LOG2E
    n = ATTN_WIN + ATTN_TQ
    far = jnp.broadcast_to(rb[:, 2 * REL_CLIP:], (ATTN_HEADS, ATTN_WIN - REL_CLIP))
    near = rb[:, REL_CLIP - ATTN_TQ + 1:2 * REL_CLIP][:, ::-1]
    x = jnp.concatenate([far, near, jnp.zeros((ATTN_HEADS, 1), F32)], axis=1)
    rolled = jnp.tile(x, (1, ATTN_TQ))[:, :ATTN_TQ * (n - 1)].reshape(ATTN_HEADS, ATTN_TQ, n - 1)
    tbl = rolled[:, :, ATTN_TQ - 1:ATTN_TQ - 1 + ATTN_WIN]
    qc = np.arange(ATTN_TQ)[:, None] // CHUNK
    kc = np.arange(ATTN_WIN)[None, :] // CHUNK
    tbl = jnp.where((kc >= qc) & (kc <= qc + LEFT_CHUNKS), tbl, -jnp.inf)
    tbl = tbl.reshape(ATTN_HEADS // 2, 2, ATTN_TQ, ATTN_WIN).transpose(0, 3, 1, 2)
    tbl = tbl.reshape(ATTN_HEADS // 2, ATTN_WIN // ATTN_TQ, ATTN_TQ, 2 * ATTN_TQ)
    return tbl[:, np.asarray(BIAS_BLOCKS)]


def kernel(x, ffn1_norm_g, ffn1_w_gate, ffn1_w_up, ffn1_w_down, mix_norm_g, w_in,
           attn_q_norm_g, attn_k_norm_g, attn_rel_bias, hgrn_lower_bounds, hgrn_out_norm_g,
           w_out, ffn2_norm_g, ffn2_w_gate, ffn2_w_up, ffn2_w_down):
    bsz, seq, _ = x.shape
    depth = ffn1_norm_g.shape[0]
    assert depth == 1 and seq % MIX_ROWS == 0 and (bsz * seq) % ROW_TILE == 0
    head_of_col = np.arange(MXU_TILE) // ATTN_HEAD_DIM
    seg = jnp.asarray(head_of_col[:, None] == head_of_col[None, :], dtype=BF16)
    lsum = jnp.asarray(_level_sum_matrix(), dtype=BF16)
    lmask = jnp.asarray(_level_masks())
    row = lambda g: g.reshape(1, -1).astype(F32)
    tile_heads = lambda g: jnp.tile(g.astype(F32), ATTN_HEADS).reshape(1, ATTN_WIDTH)

    xf = x.reshape(bsz * seq, D_MODEL)
    for l in range(depth):
        later = (w_in[l], w_out[l], ffn2_w_gate[l], ffn2_w_up[l], ffn2_w_down[l])
        x1, w_in_b, w_out_b, wg2_b, wu2_b, wd2_b = _ffn_call(
            xf, row(ffn1_norm_g[l]), ffn1_w_gate[l].astype(F32), ffn1_w_up[l].astype(F32),
            ffn1_w_down[l].astype(F32), cast=tuple(w.astype(F32) for w in later))
        mixed = _mixer_call(
            x1.reshape(bsz, seq, D_MODEL), row(mix_norm_g[l]), w_in_b,
            tile_heads(attn_q_norm_g[l]), tile_heads(attn_k_norm_g[l]), hgrn_lower_bounds.astype(F32),
            seg, _rel_bias_table(attn_rel_bias[l]), lsum, lmask, row(hgrn_out_norm_g[l]))
        xf, = _ffn_call(x1, row(ffn2_norm_g[l]), wg2_b, wu2_b, wd2_b,
                        mix=(mixed.reshape(bsz * seq, D_MODEL), w_out_b))
    return xf.reshape(bsz, seq, D_MODEL)
```

```python
import functools

import numpy as np
import jax
import jax.numpy as jnp
from jax import lax
from jax.experimental import pallas as pl
from jax.experimental.pallas import tpu as pltpu

D_MODEL = 1024
CHUNK = 64
ATTN_WIDTH = 512
HGRN_WIDTH = 512
ATTN_HEAD_DIM = 64
ATTN_HEADS = 8
HGRN_HEAD_DIM = 128
HGRN_HEADS = 4
LEFT_CHUNKS = 8
PAD = LEFT_CHUNKS * CHUNK
REL_CLIP = 128
D_FF = 2816
RMS_EPS = 1e-6
PROJ_GROUPS = 7
PROJ_COLS = PROJ_GROUPS * ATTN_WIDTH

LANES = 128
SUBLANES = 8
BF16_ROWS = 2 * SUBLANES
MXU_TILE = 256
FF_TILE = MXU_TILE
ROW_TILE = 1024
FFN_ROW_PIECES = 4
WEIGHT_SLABS = 8
WEIGHT_SLOTS = 3
LOG2E = float(np.log2(np.e))
ATTN_Q_CHUNKS = 2
ATTN_TQ = ATTN_Q_CHUNKS * CHUNK
ATTN_WIN = PAD + ATTN_TQ
MIX_ROWS = 512
N_LEVELS = 6
LEVELS = tuple(CHUNK >> (lv + 1) for lv in range(N_LEVELS))
FINE_LEVELS = tuple(c for c in LEVELS if c < SUBLANES)
VMEM_LIMIT = 56 * 1024 * 1024

BF16 = jnp.bfloat16
F32 = jnp.float32


def _dot(a, b):
    return jnp.dot(a, b, preferred_element_type=F32)


def _dot_nt(a, b):
    return lax.dot_general(a, b, (((1,), (1,)), ((), ())), preferred_element_type=F32)


def _dot_tn(a, b):
    return lax.dot_general(a, b, (((0,), (0,)), ((), ())), preferred_element_type=F32)


def _silu(x):
    return x / (1.0 + jnp.exp(-x))


def _rms(x, g):
    ms = jnp.mean(x * x, axis=-1, keepdims=True)
    return x * lax.rsqrt(ms + RMS_EPS) * g


def _resident(shape):
    return pl.BlockSpec(shape, lambda *_: (0,) * len(shape), pipeline_mode=pl.Buffered(1))


ROW_PIECES = [slice(r * ROW_TILE // FFN_ROW_PIECES, (r + 1) * ROW_TILE // FFN_ROW_PIECES)
              for r in range(FFN_ROW_PIECES)]


def _ffn_body(xs, g_ref, wg_ref, wu_ref, wd_ref, o_ref, act_ref):
    first = slice(0, FF_TILE)
    hs, gates, ups = [], [], []
    for xr in xs:
        hs.append(_rms(xr, g_ref[...]).astype(BF16))
        gates.append(_dot(hs[-1], wg_ref[:, first]))
        ups.append(_dot(hs[-1], wu_ref[:, first]))
    gate, up = jnp.concatenate(gates, axis=0), jnp.concatenate(ups, axis=0)
    act_ref[:, first] = (_silu(gate) * up).astype(BF16)
    h = jnp.concatenate(hs, axis=0)
    for j in range(1, D_FF // FF_TILE):
        cols = slice(j * FF_TILE, (j + 1) * FF_TILE)
        gate = _dot(h, wg_ref[:, cols])
        up = _dot(h, wu_ref[:, cols])
        act_ref[:, cols] = (_silu(gate) * up).astype(BF16)
    y = _dot(act_ref[...], wd_ref[...])
    o_ref[...] = jnp.concatenate(xs, axis=0) + 0.5 * y


def _load_weights_bf16(hbm_refs, vmem_refs, stage_refs, sem_ref):
    jobs = []
    for src, dst in zip(hbm_refs, vmem_refs):
        stage = stage_refs[0] if src.shape[1] == stage_refs[0].shape[2] else stage_refs[1]
        rows = stage.shape[1]
        assert src.shape[0] % rows == 0 and src.shape[1] == stage.shape[2]
        jobs += [(src, dst, stage, r0, rows) for r0 in range(0, src.shape[0], rows)]

    def copy(n):
        src, _, stage, r0, rows = jobs[n]
        slot = n % WEIGHT_SLOTS
        which = 0 if stage is stage_refs[0] else 1
        return pltpu.make_async_copy(src.at[pl.ds(r0, rows), :], stage.at[slot], sem_ref.at[which, slot])

    ahead = WEIGHT_SLOTS - 1
    for n in range(min(ahead, len(jobs))):
        copy(n).start()
    for n, (_, dst, stage, r0, rows) in enumerate(jobs):
        if n + ahead < len(jobs):
            copy(n + ahead).start()
        copy(n).wait()
        dst[pl.ds(r0, rows), :] = stage[n % WEIGHT_SLOTS].astype(BF16)


def _ffn_kernel(n_cast, x_ref, g_ref, wg_hbm, wu_hbm, wd_hbm, *rest):
    cast_in, rest = rest[:n_cast], rest[n_cast:]
    o_ref, cast_out = rest[0], rest[1:1 + n_cast]
    act_ref, wg_ref, wu_ref, wd_ref, stage_up, stage_down, sem_ref = rest[1 + n_cast:]

    @pl.when(pl.program_id(0) == 0)
    def _():
        _load_weights_bf16((wg_hbm, wu_hbm, wd_hbm), (wg_ref, wu_ref, wd_ref),
                           (stage_up, stage_down), sem_ref)

    for src, dst in zip(cast_in, cast_out):
        dst[...] = src[...].astype(BF16)
    _ffn_body([x_ref[rows, :] for rows in ROW_PIECES], g_ref, wg_ref, wu_ref, wd_ref, o_ref, act_ref)


def _mix_ffn_kernel(x_ref, m_ref, wo_ref, g_ref, wg_ref, wu_ref, wd_ref, o_ref, act_ref):
    xs = [x_ref[rows, :] + _dot(m_ref[rows, :], wo_ref[...]) for rows in ROW_PIECES]
    _ffn_body(xs, g_ref, wg_ref, wu_ref, wd_ref, o_ref, act_ref)


def _rows(width):
    return pl.BlockSpec((ROW_TILE, width), lambda i: (i, 0))


def _slab_spec(shape, steps):
    rows, width = shape
    per = 1 if (rows // steps) % BF16_ROWS == 0 and rows % steps == 0 else 2
    slab = rows * per // steps
    assert slab * steps == rows * per and slab % BF16_ROWS == 0
    return pl.BlockSpec((slab, width), lambda i: (i // per, 0))


def _ffn_call(x, g, wg, wu, wd, mix=None, cast=()):
    m = x.shape[0]
    steps = m // ROW_TILE
    w_specs = [_resident((1, D_MODEL)), _resident((D_MODEL, D_FF)),
               _resident((D_MODEL, D_FF)), _resident((D_FF, D_MODEL))]
    out_specs, out_shape = [_rows(D_MODEL)], [jax.ShapeDtypeStruct((m, D_MODEL), F32)]
    scratch = [pltpu.VMEM((ROW_TILE, D_FF), BF16)]
    if mix is None:
        kern = functools.partial(_ffn_kernel, len(cast))
        ins = (x, g, wg, wu, wd) + tuple(cast)
        in_hbm = pl.BlockSpec(memory_space=pl.ANY)
        specs = ([_rows(D_MODEL), w_specs[0], in_hbm, in_hbm, in_hbm]
                 + [_slab_spec(w.shape, steps) for w in cast])
        out_specs += [_slab_spec(w.shape, steps) for w in cast]
        out_shape += [jax.ShapeDtypeStruct(w.shape, BF16) for w in cast]
        scratch += [pltpu.VMEM(wg.shape, BF16), pltpu.VMEM(wu.shape, BF16), pltpu.VMEM(wd.shape, BF16),
                    pltpu.VMEM((WEIGHT_SLOTS, D_MODEL // WEIGHT_SLABS, D_FF), F32),
                    pltpu.VMEM((WEIGHT_SLOTS, D_FF // WEIGHT_SLABS, D_MODEL), F32),
                    pltpu.SemaphoreType.DMA((2, WEIGHT_SLOTS))]
    else:
        assert not cast
        mixed, wo = mix
        kern = _mix_ffn_kernel
        ins = (x, mixed, wo, g, wg, wu, wd)
        specs = [_rows(D_MODEL), _rows(D_MODEL), _resident((D_MODEL, D_MODEL))] + w_specs
    return pl.pallas_call(
        kern,
        grid=(steps,),
        in_specs=specs,
        out_specs=out_specs,
        out_shape=out_shape,
        scratch_shapes=scratch,
        compiler_params=pltpu.CompilerParams(
            dimension_semantics=("arbitrary",), vmem_limit_bytes=VMEM_LIMIT),
        name="mix_ffn" if mix is not None else "ffn",
    )(*ins)


def _projection_stages(x_ref, g_ref, w_ref, qg_ref, kg_ref, lbp_ref, seg_ref, h_ref, dst):
    names = ("q", "k", "v", "rq", "forget", "rv", "sg")
    per_group = ATTN_WIDTH // MXU_TILE

    def normalize():
        h_ref[...] = _rms(x_ref[0], g_ref[...]).astype(BF16)

    def project(i):
        return _dot(h_ref[...], w_ref[:, i * MXU_TILE:(i + 1) * MXU_TILE])

    def head_normed(a, gain):
        ms = _dot((a * a).astype(BF16), seg_ref[...]) * (1.0 / ATTN_HEAD_DIM)
        return (a * lax.rsqrt(ms + RMS_EPS) * gain).astype(BF16)

    def tail(i, val):
        group, part = divmod(i, per_group)
        cols = slice(part * MXU_TILE, (part + 1) * MXU_TILE)
        name = names[group]
        if name == "q":
            dst["q"](cols, head_normed(val, qg_ref[:, cols] * (ATTN_HEAD_DIM ** -0.5 * LOG2E)))
        elif name == "k":
            dst["k"](cols, head_normed(val, kg_ref[:, cols]))
        elif name in ("v", "rv"):
            dst[name](cols, val.astype(BF16))
        elif name in ("rq", "sg"):
            dst[name](cols, _silu(val).astype(BF16))
        else:
            lbp = lbp_ref[:, cols]
            e = jnp.exp(lbp - jnp.max(lbp, axis=0, keepdims=True))
            lb = e[0:1, :] / jnp.sum(e, axis=0, keepdims=True)
            en = jnp.exp(-jnp.abs(val))
            big, small = 1.0 / (1.0 + en), en / (1.0 + en)
            pos = val >= 0
            f = lb + (1.0 - lb) * jnp.where(pos, big, small)
            dst["lf"](cols, jnp.log2(f).astype(BF16))
            dst["rk"](cols, ((1.0 - lb) * jnp.where(pos, small, big)).astype(BF16))

    return PROJ_GROUPS * per_group, normalize, (project, tail)


def _attention_stages(tile, src, bias_ref, o_ref):
    n_pairs = ATTN_HEADS // 2
    groups = MIX_ROWS // ATTN_TQ

    def where(u):
        g, p = divmod(u, n_pairs)
        group = tile * groups + g
        window = pl.ds(pl.multiple_of(group * ATTN_TQ, ATTN_TQ), ATTN_WIN)
        return group, window, slice(g * ATTN_TQ, (g + 1) * ATTN_TQ), slice(p * LANES, (p + 1) * LANES), p

    def scores(u):
        group, window, rows, cols, p = where(u)
        qt = src["qt"](cols, rows)
        zero = jnp.zeros((ATTN_HEAD_DIM, ATTN_TQ), BF16)
        q2 = jnp.concatenate([jnp.concatenate([qt[:ATTN_HEAD_DIM], zero], axis=0),
                              jnp.concatenate([zero, qt[ATTN_HEAD_DIM:]], axis=0)], axis=1)
        st = _dot(src["k"](window, cols), q2)
        first_valid = PAD // ATTN_TQ - group
        blocks = []
        for j in range(ATTN_WIN // ATTN_TQ):
            sj = st[j * ATTN_TQ:(j + 1) * ATTN_TQ]
            if j in BIAS_BLOCKS:
                sj = sj + bias_ref[p, BIAS_BLOCKS.index(j)]
            blocks.append(jnp.where(j >= first_valid, sj, -jnp.inf))
        return jnp.concatenate(blocks, axis=0)

    def weigh(u, st):
        _, window, rows, cols, _ = where(u)
        e = jnp.exp2(st - jnp.max(st, axis=0, keepdims=True)).astype(BF16)
        vt1 = jnp.concatenate([src["vt"](cols, window), src["ones"](window)], axis=0)
        ot = _dot(vt1, e)
        ot = ot[:LANES] / ot[LANES:LANES + 1]
        same_head = jnp.concatenate([ot[:ATTN_HEAD_DIM, :ATTN_TQ], ot[ATTN_HEAD_DIM:, ATTN_TQ:]], axis=0)
        o_ref[0, rows, cols] = same_head.astype(BF16).T

    return groups * n_pairs, (scores, weigh)


def _level_sum_matrix():
    u = np.arange(CHUNK)[:, None]
    r = np.arange(CHUNK)[None, :]
    mats = [(r <= u)]
    for c in FINE_LEVELS:
        m = (u // (2 * c)) * (2 * c) + c
        upper = (u & c) != 0
        mats.append(np.where(upper, (r > m) & (r <= u), (r > u) & (r <= m)))
    return np.concatenate(mats, axis=0).astype(np.float32)


def _level_masks():
    t = np.arange(CHUNK)[:, None]
    s = np.arange(CHUNK)[None, :]
    return np.stack([t == s] + [(t ^ s) < 2 * c for c in LEVELS]).astype(np.float32)


def _hgrn_stages(src, lsum_ref, lmask_ref, og_ref, o_ref, state_ref):
    t_row = lax.broadcasted_iota(jnp.int32, (CHUNK, HGRN_HEAD_DIM), 0)
    esums = {}

    def where(n):
        u, hd = divmod(n, HGRN_HEADS)
        return u, slice(u * CHUNK, (u + 1) * CHUNK), slice(hd * HGRN_HEAD_DIM, (hd + 1) * HGRN_HEAD_DIM), hd

    def coarse_operands(q, k, b, c):
        qs, ks = [], []
        zero = jnp.zeros((c, HGRN_HEAD_DIM), F32)
        for blk in range(CHUNK // c):
            r = slice(blk * c, (blk + 1) * c)
            m = (blk // 2) * 2 * c + c
            b_m = b[m:m + 1, :]
            qs.append(q[r] * jnp.exp2(b[r] - b_m) if blk % 2 else zero)
            ks.append(zero if blk % 2 else k[r] * jnp.exp2(b_m - b[r]))
        return jnp.concatenate(qs, axis=0).astype(BF16), jnp.concatenate(ks, axis=0).astype(BF16)

    def fine_operands(q, k, w, c):
        upper = (t_row & c) != 0
        return (jnp.where(upper, q * w, 0.0).astype(BF16), jnp.where(upper, 0.0, k * w).astype(BF16))

    def products(n):
        u, rows, cols, _ = where(n)
        if u not in esums:
            esums[u] = _dot(lsum_ref[...], src["lf"](rows, slice(None)))
        esum = esums[u]
        q = src["rq"](rows, cols).astype(F32)
        k = src["rk"](rows, cols).astype(F32)
        b = esum[:CHUNK, cols]
        b_last = b[CHUNK - 1:CHUNK, :]
        prods = []
        for c in LEVELS:
            if c in FINE_LEVELS:
                i = 1 + FINE_LEVELS.index(c)
                qw, kw = fine_operands(q, k, jnp.exp2(esum[i * CHUNK:(i + 1) * CHUNK, cols]), c)
            else:
                qw, kw = coarse_operands(q, k, b, c)
            prods.append(_dot_nt(qw, kw))
        return dict(q_in=(q * jnp.exp2(b)).astype(BF16), k_up=(k * jnp.exp2(b_last - b)).astype(BF16),
                    decay=jnp.exp2(b_last), diag=jnp.sum(q * k, axis=-1, keepdims=True), prods=prods)

    def finish(n, a):
        _, rows, cols, hd = where(n)
        scores = a["diag"] * lmask_ref[0]
        for lv in range(N_LEVELS):
            scores = scores + a["prods"][lv] * lmask_ref[1 + lv]
        v = src["rv"](rows, cols)
        state = state_ref[hd]
        o = _dot_nt(a["q_in"], state.astype(BF16)) + _dot(scores.astype(BF16), v)
        state_ref[hd] = state * a["decay"] + _dot_tn(v, a["k_up"])
        y = _rms(o, og_ref[...]) * src["sg"](rows, cols).astype(F32)
        out_cols = slice(ATTN_WIDTH + hd * HGRN_HEAD_DIM, ATTN_WIDTH + (hd + 1) * HGRN_HEAD_DIM)
        o_ref[0, rows, out_cols] = y.astype(BF16)

    return (MIX_ROWS // CHUNK) * HGRN_HEADS, (products, finish)


def _mixer_kernel(x0_ref, xn_ref, g_ref, w_ref, qg_ref, kg_ref, lbp_ref, seg_ref, bias_ref, lsum_ref,
                  lmask_ref, og_ref, o_ref, h_ref, qt_ref, rq_ref, rk_ref, rv_ref, sg_ref, lf_ref,
                  kpad_ref, vt_ref, ones_ref, state_ref):
    seq = kpad_ref.shape[1] - PAD
    tiles = seq // MIX_ROWS
    s = pl.program_id(0)
    tile = lax.rem(s, tiles)
    nxt = jnp.minimum(s + 1, pl.num_programs(0) - 1)

    def projection(x_ref, step):
        slot = lax.rem(step, 2)
        par = lax.rem(lax.div(step, tiles), 2)
        rows = pl.ds(pl.multiple_of(PAD + lax.rem(step, tiles) * MIX_ROWS, MIX_ROWS), MIX_ROWS)

        def put(ref):
            def store(cols, val):
                ref[slot, :, cols] = val
            return store

        def put_k(cols, val):
            kpad_ref[par, rows, cols] = val

        def put_v(cols, val):
            vt_ref[par, cols, rows] = val.T

        def put_q(cols, val):
            qt_ref[slot, cols, :] = val.T

        dst = dict(q=put_q, k=put_k, v=put_v, rq=put(rq_ref), rk=put(rk_ref), rv=put(rv_ref),
                   sg=put(sg_ref), lf=put(lf_ref))
        return _projection_stages(x_ref, g_ref, w_ref, qg_ref, kg_ref, lbp_ref, seg_ref, h_ref, dst)

    @pl.when(s == 0)
    def _():
        kpad_ref[:, :PAD, :] = jnp.zeros((2, PAD, ATTN_WIDTH), BF16)
        vt_ref[:, :, :PAD] = jnp.zeros((2, ATTN_WIDTH, PAD), BF16)
        ones_ref[:, :PAD] = jnp.zeros((BF16_ROWS, PAD), BF16)
        ones_ref[:, PAD:] = jnp.ones((BF16_ROWS, seq), BF16)
        n_units, normalize, (project, tail) = projection(x0_ref, s)
        normalize()
        nxt_val = project(0)
        for i in range(n_units):
            cur, nxt_val = nxt_val, (project(i + 1) if i + 1 < n_units else None)
            tail(i, cur)

    @pl.when(tile == 0)
    def _():
        state_ref[...] = jnp.zeros_like(state_ref)

    slot = lax.rem(s, 2)
    par = lax.rem(lax.div(s, tiles), 2)
    tile_src = lambda ref: (lambda rows, cols: ref[slot, rows, cols])
    src = dict(qt=tile_src(qt_ref), rq=tile_src(rq_ref), rk=tile_src(rk_ref), rv=tile_src(rv_ref),
               sg=tile_src(sg_ref), lf=tile_src(lf_ref),
               k=lambda window, cols: kpad_ref[par, window, cols],
               vt=lambda cols, window: vt_ref[par, cols, window],
               ones=lambda window: ones_ref[:, window])
    n_proj, normalize, (proj_first, proj_second) = projection(xn_ref, nxt)
    n_attn, (attn_first, attn_second) = _attention_stages(tile, src, bias_ref, o_ref)
    n_hgrn, (hgrn_first, hgrn_second) = _hgrn_stages(src, lsum_ref, lmask_ref, og_ref, o_ref, state_ref)

    assert n_proj <= n_attn and n_hgrn % n_attn == 0
    per = n_hgrn // n_attn
    normalize()
    p_next, a_next, h_next = proj_first(0), attn_first(0), hgrn_first(0)
    for i in range(n_attn):
        a_cur, a_next = a_next, (attn_first(i + 1) if i + 1 < n_attn else None)
        for n in range(i * per, (i + 1) * per):
            h_cur, h_next = h_next, (hgrn_first(n + 1) if n + 1 < n_hgrn else None)
            hgrn_second(n, h_cur)
            if n == i * per:
                attn_second(i, a_cur)
                if i < n_proj:
                    p_cur, p_next = p_next, (proj_first(i + 1) if i + 1 < n_proj else None)
                    proj_second(i, p_cur)


def _mixer_call(x1, g, w_in, qg, kg, lbp, seg, bias, lsum, lmask, og):
    b, s, _ = x1.shape
    tiles = s // MIX_ROWS
    steps = b * tiles
    slot_pair = pltpu.VMEM((2, MIX_ROWS, HGRN_WIDTH), BF16)

    def next_tile(i):
        n = jnp.minimum(i + 1, steps - 1)
        return (n // tiles, n % tiles, 0)

    return pl.pallas_call(
        _mixer_kernel,
        grid=(steps,),
        in_specs=[pl.BlockSpec((1, MIX_ROWS, D_MODEL), lambda i: (0, 0, 0), pipeline_mode=pl.Buffered(1)),
                  pl.BlockSpec((1, MIX_ROWS, D_MODEL), next_tile),
                  _resident((1, D_MODEL)), _resident((D_MODEL, PROJ_COLS)),
                  _resident((1, ATTN_WIDTH)), _resident((1, ATTN_WIDTH)), _resident(lbp.shape),
                  _resident((MXU_TILE, MXU_TILE)), _resident(bias.shape), _resident(lsum.shape),
                  _resident(lmask.shape), _resident((1, HGRN_HEAD_DIM))],
        out_specs=pl.BlockSpec((1, MIX_ROWS, D_MODEL), lambda i: (i // tiles, i % tiles, 0)),
        out_shape=jax.ShapeDtypeStruct((b, s, D_MODEL), BF16),
        scratch_shapes=[pltpu.VMEM((MIX_ROWS, D_MODEL), BF16),
                        pltpu.VMEM((2, ATTN_WIDTH, MIX_ROWS), BF16)] + [slot_pair] * 5 + [
                        pltpu.VMEM((2, s + PAD, ATTN_WIDTH), BF16),
                        pltpu.VMEM((2, ATTN_WIDTH, s + PAD), BF16),
                        pltpu.VMEM((BF16_ROWS, s + PAD), BF16),
                        pltpu.VMEM((HGRN_HEADS, HGRN_HEAD_DIM, HGRN_HEAD_DIM), F32)],
        compiler_params=pltpu.CompilerParams(
            dimension_semantics=("arbitrary",), vmem_limit_bytes=VMEM_LIMIT),
        name="mixer",
    )(x1, x1, g, w_in, qg, kg, lbp, seg, bias, lsum, lmask, og)


def _bias_blocks():
    qq = np.arange(ATTN_TQ)[:, None]
    key = np.arange(ATTN_WIN)[None, :]
    near = PAD + qq - key < REL_CLIP
    band = (key // CHUNK >= qq // CHUNK) & (key // CHUNK <= qq // CHUNK + LEFT_CHUNKS)
    used = (near | ~band).any(axis=0).reshape(ATTN_WIN // ATTN_TQ, ATTN_TQ).any(axis=1)
    return tuple(int(j) for j in np.nonzero(used)[0])


BIAS_BLOCKS = _bias_blocks()


def _rel_bias_table(rel_bias):
    assert ATTN_TQ - 1 <= REL_CLIP
    rb = (rel_bias.astype(F32) - rel_bias.astype(F32)[:, 2 * REL_CLIP:]) * LOG2E
    qc = np.arange(ATTN_TQ)[:, None] // CHUNK
    kc = np.arange(ATTN_WIN)[None, :] // CHUNK
    band = (kc >= qc) & (kc <= qc + LEFT_CHUNKS)
    k0 = (PAD - REL_CLIP) // ATTN_TQ * ATTN_TQ
    w = ATTN_WIN - k0
    n = w + ATTN_TQ
    far = jnp.zeros((ATTN_HEADS, w - REL_CLIP), F32)
    near = rb[:, REL_CLIP - ATTN_TQ + 1:2 * REL_CLIP][:, ::-1]
    x = jnp.concatenate([far, near, jnp.zeros((ATTN_HEADS, 1), F32)], axis=1)
    rolled = jnp.tile(x, (1, ATTN_TQ))[:, :ATTN_TQ * (n - 1)].reshape(ATTN_HEADS, ATTN_TQ, n - 1)
    tbl = rolled[:, :, ATTN_TQ - 1:ATTN_TQ - 1 + w]
    tbl = jnp.where(band[:, k0:], tbl, -jnp.inf)
    tbl = tbl.reshape(ATTN_HEADS // 2, 2, ATTN_TQ, w).transpose(0, 3, 1, 2)
    tbl = tbl.reshape(ATTN_HEADS // 2, w // ATTN_TQ, ATTN_TQ, 2 * ATTN_TQ)
    blocks = []
    for j in BIAS_BLOCKS:
        if j * ATTN_TQ >= k0:
            blocks.append(tbl[:, j - k0 // ATTN_TQ])
        else:
            keys = slice(j * ATTN_TQ, (j + 1) * ATTN_TQ)
            mask = np.where(band[:, keys], 0.0, -np.inf).astype(np.float32).T
            blocks.append(jnp.broadcast_to(jnp.asarray(np.tile(mask, (1, 2))),
                                           (ATTN_HEADS // 2, ATTN_TQ, 2 * ATTN_TQ)))
    return jnp.stack(blocks, axis=1)


def kernel(x, ffn1_norm_g, ffn1_w_gate, ffn1_w_up, ffn1_w_down, mix_norm_g, w_in,
           attn_q_norm_g, attn_k_norm_g, attn_rel_bias, hgrn_lower_bounds, hgrn_out_norm_g,
           w_out, ffn2_norm_g, ffn2_w_gate, ffn2_w_up, ffn2_w_down):
    bsz, seq, _ = x.shape
    depth = ffn1_norm_g.shape[0]
    assert depth == 1 and seq % MIX_ROWS == 0 and (bsz * seq) % ROW_TILE == 0
    head_of_col = np.arange(MXU_TILE) // ATTN_HEAD_DIM
    seg = jnp.asarray(head_of_col[:, None] == head_of_col[None, :], dtype=BF16)
    lsum = jnp.asarray(_level_sum_matrix(), dtype=BF16)
    lmask = jnp.asarray(_level_masks())
    row = lambda g: g.reshape(1, -1).astype(F32)
    tile_heads = lambda g: jnp.tile(g.astype(F32), ATTN_HEADS).reshape(1, ATTN_WIDTH)

    xf = x.reshape(bsz * seq, D_MODEL)
    for l in range(depth):
        later = (w_in[l], w_out[l], ffn2_w_gate[l], ffn2_w_up[l], ffn2_w_down[l])
        x1, w_in_b, w_out_b, wg2_b, wu2_b, wd2_b = _ffn_call(
            xf, row(ffn1_norm_g[l]), ffn1_w_gate[l].astype(F32), ffn1_w_up[l].astype(F32),
            ffn1_w_down[l].astype(F32), cast=tuple(w.astype(F32) for w in later))
        mixed = _mixer_call(
            x1.reshape(bsz, seq, D_MODEL), row(mix_norm_g[l]), w_in_b,
            tile_heads(attn_q_norm_g[l]), tile_heads(attn_k_norm_g[l]), hgrn_lower_bounds.astype(F32),
            seg, _rel_bias_table(attn_rel_bias[l]), lsum, lmask, row(hgrn_out_norm_g[l]))
        xf, = _ffn_call(x1, row(ffn2_norm_g[l]), wg2_b, wu2_b, wd2_b,
                        mix=(mixed.reshape(bsz * seq, D_MODEL), w_out_b))
    return xf.reshape(bsz, seq, D_MODEL)
```

```python
import functools

import numpy as np
import jax
import jax.numpy as jnp
from jax import lax
from jax.experimental import pallas as pl
from jax.experimental.pallas import tpu as pltpu

D_MODEL = 1024
CHUNK = 64
ATTN_WIDTH = 512
HGRN_WIDTH = 512
ATTN_HEAD_DIM = 64
ATTN_HEADS = 8
HGRN_HEAD_DIM = 128
HGRN_HEADS = 4
LEFT_CHUNKS = 8
PAD = LEFT_CHUNKS * CHUNK
REL_CLIP = 128
D_FF = 2816
RMS_EPS = 1e-6
PROJ_GROUPS = 7
PROJ_COLS = PROJ_GROUPS * ATTN_WIDTH

LANES = 128
SUBLANES = 8
BF16_ROWS = 2 * SUBLANES
MXU_TILE = 256
FF_TILE = MXU_TILE
ROW_TILE = 1024
FFN_ROW_PIECES = 4
WEIGHT_SLABS = 8
WEIGHT_SLOTS = 3
LOG2E = float(np.log2(np.e))
ATTN_Q_CHUNKS = 2
ATTN_TQ = ATTN_Q_CHUNKS * CHUNK
ATTN_WIN = PAD + ATTN_TQ
MIX_ROWS = 512
N_LEVELS = 6
LEVELS = tuple(CHUNK >> (lv + 1) for lv in range(N_LEVELS))
FINE_LEVELS = tuple(c for c in LEVELS if c < SUBLANES)
VMEM_LIMIT = 56 * 1024 * 1024

BF16 = jnp.bfloat16
F32 = jnp.float32


def _dot(a, b):
    return jnp.dot(a, b, preferred_element_type=F32)


def _dot_nt(a, b):
    return lax.dot_general(a, b, (((1,), (1,)), ((), ())), preferred_element_type=F32)


def _dot_tn(a, b):
    return lax.dot_general(a, b, (((0,), (0,)), ((), ())), preferred_element_type=F32)


def _silu(x):
    return x / (1.0 + jnp.exp(-x))


def _rms(x, g):
    ms = jnp.mean(x * x, axis=-1, keepdims=True)
    return x * lax.rsqrt(ms + RMS_EPS) * g


def _resident(shape):
    return pl.BlockSpec(shape, lambda *_: (0,) * len(shape), pipeline_mode=pl.Buffered(1))


ROW_PIECES = [slice(r * ROW_TILE // FFN_ROW_PIECES, (r + 1) * ROW_TILE // FFN_ROW_PIECES)
              for r in range(FFN_ROW_PIECES)]


def _ffn_body(xs, g_ref, wg_ref, wu_ref, wd_ref, o_ref, act_ref):
    first = slice(0, FF_TILE)
    hs, gates, ups = [], [], []
    for xr in xs:
        hs.append(_rms(xr, g_ref[...]).astype(BF16))
        gates.append(_dot(hs[-1], wg_ref[:, first]))
        ups.append(_dot(hs[-1], wu_ref[:, first]))
    gate, up = jnp.concatenate(gates, axis=0), jnp.concatenate(ups, axis=0)
    act_ref[:, first] = (_silu(gate) * up).astype(BF16)
    h = jnp.concatenate(hs, axis=0)
    for j in range(1, D_FF // FF_TILE):
        cols = slice(j * FF_TILE, (j + 1) * FF_TILE)
        gate = _dot(h, wg_ref[:, cols])
        up = _dot(h, wu_ref[:, cols])
        act_ref[:, cols] = (_silu(gate) * up).astype(BF16)
    y = _dot(act_ref[...], wd_ref[...])
    o_ref[...] = jnp.concatenate(xs, axis=0) + 0.5 * y


def _load_weights_bf16(hbm_refs, vmem_refs, stage_refs, sem_ref):
    jobs = []
    for src, dst in zip(hbm_refs, vmem_refs):
        stage = stage_refs[0] if src.shape[1] == stage_refs[0].shape[2] else stage_refs[1]
        rows = stage.shape[1]
        assert src.shape[0] % rows == 0 and src.shape[1] == stage.shape[2]
        jobs += [(src, dst, stage, r0, rows) for r0 in range(0, src.shape[0], rows)]

    def copy(n):
        src, _, stage, r0, rows = jobs[n]
        slot = n % WEIGHT_SLOTS
        which = 0 if stage is stage_refs[0] else 1
        return pltpu.make_async_copy(src.at[pl.ds(r0, rows), :], stage.at[slot], sem_ref.at[which, slot])

    ahead = WEIGHT_SLOTS - 1
    for n in range(min(ahead, len(jobs))):
        copy(n).start()
    for n, (_, dst, stage, r0, rows) in enumerate(jobs):
        if n + ahead < len(jobs):
            copy(n + ahead).start()
        copy(n).wait()
        dst[pl.ds(r0, rows), :] = stage[n % WEIGHT_SLOTS].astype(BF16)


def _ffn_kernel(n_cast, x_ref, g_ref, wg_hbm, wu_hbm, wd_hbm, *rest):
    cast_in, rest = rest[:n_cast], rest[n_cast:]
    o_ref, cast_out = rest[0], rest[1:1 + n_cast]
    act_ref, wg_ref, wu_ref, wd_ref, stage_up, stage_down, sem_ref = rest[1 + n_cast:]

    @pl.when(pl.program_id(0) == 0)
    def _():
        _load_weights_bf16((wg_hbm, wu_hbm, wd_hbm), (wg_ref, wu_ref, wd_ref),
                           (stage_up, stage_down), sem_ref)

    for src, dst in zip(cast_in, cast_out):
        dst[...] = src[...].astype(BF16)
    _ffn_body([x_ref[rows, :] for rows in ROW_PIECES], g_ref, wg_ref, wu_ref, wd_ref, o_ref, act_ref)


def _mix_ffn_kernel(x_ref, m_ref, wo_ref, g_ref, wg_ref, wu_ref, wd_ref, o_ref, act_ref):
    xs = [x_ref[rows, :] + _dot(m_ref[rows, :], wo_ref[...]) for rows in ROW_PIECES]
    _ffn_body(xs, g_ref, wg_ref, wu_ref, wd_ref, o_ref, act_ref)


def _rows(width):
    return pl.BlockSpec((ROW_TILE, width), lambda i: (i, 0))


def _slab_spec(shape, steps):
    rows, width = shape
    per = 1 if (rows // steps) % BF16_ROWS == 0 and rows % steps == 0 else 2
    slab = rows * per // steps
    assert slab * steps == rows * per and slab % BF16_ROWS == 0
    return pl.BlockSpec((slab, width), lambda i: (i // per, 0))


def _ffn_call(x, g, wg, wu, wd, mix=None, cast=()):
    m = x.shape[0]
    steps = m // ROW_TILE
    w_specs = [_resident((1, D_MODEL)), _resident((D_MODEL, D_FF)),
               _resident((D_MODEL, D_FF)), _resident((D_FF, D_MODEL))]
    out_specs, out_shape = [_rows(D_MODEL)], [jax.ShapeDtypeStruct((m, D_MODEL), F32)]
    scratch = [pltpu.VMEM((ROW_TILE, D_FF), BF16)]
    if mix is None:
        kern = functools.partial(_ffn_kernel, len(cast))
        ins = (x, g, wg, wu, wd) + tuple(cast)
        in_hbm = pl.BlockSpec(memory_space=pl.ANY)
        specs = ([_rows(D_MODEL), w_specs[0], in_hbm, in_hbm, in_hbm]
                 + [_slab_spec(w.shape, steps) for w in cast])
        out_specs += [_slab_spec(w.shape, steps) for w in cast]
        out_shape += [jax.ShapeDtypeStruct(w.shape, BF16) for w in cast]
        scratch += [pltpu.VMEM(wg.shape, BF16), pltpu.VMEM(wu.shape, BF16), pltpu.VMEM(wd.shape, BF16),
                    pltpu.VMEM((WEIGHT_SLOTS, D_MODEL // WEIGHT_SLABS, D_FF), F32),
                    pltpu.VMEM((WEIGHT_SLOTS, D_FF // WEIGHT_SLABS, D_MODEL), F32),
                    pltpu.SemaphoreType.DMA((2, WEIGHT_SLOTS))]
    else:
        assert not cast
        mixed, wo = mix
        kern = _mix_ffn_kernel
        ins = (x, mixed, wo, g, wg, wu, wd)
        specs = [_rows(D_MODEL), _rows(D_MODEL), _resident((D_MODEL, D_MODEL))] + w_specs
    return pl.pallas_call(
        kern,
        grid=(steps,),
        in_specs=specs,
        out_specs=out_specs,
        out_shape=out_shape,
        scratch_shapes=scratch,
        compiler_params=pltpu.CompilerParams(
            dimension_semantics=("arbitrary",), vmem_limit_bytes=VMEM_LIMIT),
        name="mix_ffn" if mix is not None else "ffn",
    )(*ins)


def _projection_stages(x_ref, g_ref, w_ref, qg_ref, kg_ref, lbp_ref, seg_ref, h_ref, dst):
    names = ("q", "k", "v", "rq", "forget", "rv", "sg")
    per_group = ATTN_WIDTH // MXU_TILE

    def normalize():
        h_ref[...] = _rms(x_ref[0], g_ref[...]).astype(BF16)

    def project(i):
        return _dot(h_ref[...], w_ref[:, i * MXU_TILE:(i + 1) * MXU_TILE])

    def head_normed(a, gain):
        ms = _dot((a * a).astype(BF16), seg_ref[...]) * (1.0 / ATTN_HEAD_DIM)
        return (a * lax.rsqrt(ms + RMS_EPS) * gain).astype(BF16)

    def tail(i, val):
        group, part = divmod(i, per_group)
        cols = slice(part * MXU_TILE, (part + 1) * MXU_TILE)
        name = names[group]
        if name == "q":
            dst["q"](cols, head_normed(val, qg_ref[:, cols] * (ATTN_HEAD_DIM ** -0.5 * LOG2E)))
        elif name == "k":
            dst["k"](cols, head_normed(val, kg_ref[:, cols]))
        elif name in ("v", "rv"):
            dst[name](cols, val.astype(BF16))
        elif name in ("rq", "sg"):
            dst[name](cols, _silu(val).astype(BF16))
        else:
            lbp = lbp_ref[:, cols]
            e = jnp.exp(lbp - jnp.max(lbp, axis=0, keepdims=True))
            lb = e[0:1, :] / jnp.sum(e, axis=0, keepdims=True)
            en = jnp.exp(-jnp.abs(val))
            big, small = 1.0 / (1.0 + en), en / (1.0 + en)
            pos = val >= 0
            f = lb + (1.0 - lb) * jnp.where(pos, big, small)
            dst["lf"](cols, jnp.log2(f).astype(BF16))
            dst["rk"](cols, ((1.0 - lb) * jnp.where(pos, small, big)).astype(BF16))

    return PROJ_GROUPS * per_group, normalize, (project, tail)


def _attention_stages(tile, src, bias_ref, o_ref):
    n_pairs = ATTN_HEADS // 2
    groups = MIX_ROWS // ATTN_TQ

    def where(u):
        g, p = divmod(u, n_pairs)
        group = tile * groups + g
        window = pl.ds(pl.multiple_of(group * ATTN_TQ, ATTN_TQ), ATTN_WIN)
        return group, window, slice(g * ATTN_TQ, (g + 1) * ATTN_TQ), slice(p * LANES, (p + 1) * LANES), p

    def scores(u):
        group, window, rows, cols, p = where(u)
        qt = src["qt"](cols, rows)
        zero = jnp.zeros((ATTN_HEAD_DIM, ATTN_TQ), BF16)
        q2 = jnp.concatenate([jnp.concatenate([qt[:ATTN_HEAD_DIM], zero], axis=0),
                              jnp.concatenate([zero, qt[ATTN_HEAD_DIM:]], axis=0)], axis=1)
        st = _dot(src["k"](window, cols), q2)
        first_valid = PAD // ATTN_TQ - group
        blocks = []
        for j in range(ATTN_WIN // ATTN_TQ):
            sj = st[j * ATTN_TQ:(j + 1) * ATTN_TQ]
            if j in BIAS_BLOCKS:
                sj = sj + bias_ref[p, BIAS_BLOCKS.index(j)]
            blocks.append(jnp.where(j >= first_valid, sj, -jnp.inf))
        return jnp.concatenate(blocks, axis=0)

    def weigh(u, st):
        _, window, rows, cols, _ = where(u)
        e = jnp.exp2(st - jnp.max(st, axis=0, keepdims=True)).astype(BF16)
        vt1 = jnp.concatenate([src["vt"](cols, window), src["ones"](window)], axis=0)
        ot = _dot(vt1, e)
        ot = ot[:LANES] / ot[LANES:LANES + 1]
        same_head = jnp.concatenate([ot[:ATTN_HEAD_DIM, :ATTN_TQ], ot[ATTN_HEAD_DIM:, ATTN_TQ:]], axis=0)
        o_ref[0, rows, cols] = same_head.astype(BF16).T

    return groups * n_pairs, (scores, weigh)


def _level_sum_matrix():
    u = np.arange(CHUNK)[:, None]
    r = np.arange(CHUNK)[None, :]
    mats = [(r <= u)]
    for c in FINE_LEVELS:
        m = (u // (2 * c)) * (2 * c) + c
        upper = (u & c) != 0
        mats.append(np.where(upper, (r > m) & (r <= u), (r > u) & (r <= m)))
    return np.concatenate(mats, axis=0).astype(np.float32)


def _level_masks():
    t = np.arange(CHUNK)[:, None]
    s = np.arange(CHUNK)[None, :]
    return np.stack([t == s] + [(t ^ s) < 2 * c for c in LEVELS]).astype(np.float32)


def _hgrn_stages(src, lsum_ref, lmask_ref, og_ref, o_ref, state_ref):
    t_row = lax.broadcasted_iota(jnp.int32, (CHUNK, HGRN_HEAD_DIM), 0)
    esums = {}

    def where(n):
        u, hd = divmod(n, HGRN_HEADS)
        return u, slice(u * CHUNK, (u + 1) * CHUNK), slice(hd * HGRN_HEAD_DIM, (hd + 1) * HGRN_HEAD_DIM), hd

    def coarse_operands(q, k, b, c):
        qs, ks = [], []
        zero = jnp.zeros((c, HGRN_HEAD_DIM), F32)
        for blk in range(CHUNK // c):
            r = slice(blk * c, (blk + 1) * c)
            m = (blk // 2) * 2 * c + c
            b_m = b[m:m + 1, :]
            qs.append(q[r] * jnp.exp2(b[r] - b_m) if blk % 2 else zero)
            ks.append(zero if blk % 2 else k[r] * jnp.exp2(b_m - b[r]))
        return jnp.concatenate(qs, axis=0).astype(BF16), jnp.concatenate(ks, axis=0).astype(BF16)

    def fine_operands(q, k, w, c):
        upper = (t_row & c) != 0
        return (jnp.where(upper, q * w, 0.0).astype(BF16), jnp.where(upper, 0.0, k * w).astype(BF16))

    def products(n):
        u, rows, cols, _ = where(n)
        if u not in esums:
            esums[u] = _dot(lsum_ref[...], src["lf"](rows, slice(None)))
        esum = esums[u]
        q = src["rq"](rows, cols).astype(F32)
        k = src["rk"](rows, cols).astype(F32)
        b = esum[:CHUNK, cols]
        b_last = b[CHUNK - 1:CHUNK, :]
        prods = []
        for c in LEVELS:
            if c in FINE_LEVELS:
                i = 1 + FINE_LEVELS.index(c)
                qw, kw = fine_operands(q, k, jnp.exp2(esum[i * CHUNK:(i + 1) * CHUNK, cols]), c)
            else:
                qw, kw = coarse_operands(q, k, b, c)
            prods.append(_dot_nt(qw, kw))
        return dict(q_in=(q * jnp.exp2(b)).astype(BF16), k_up=(k * jnp.exp2(b_last - b)).astype(BF16),
                    decay=jnp.exp2(b_last), diag=jnp.sum(q * k, axis=-1, keepdims=True), prods=prods)

    def finish(n, a):
        _, rows, cols, hd = where(n)
        scores = a["diag"] * lmask_ref[0]
        for lv in range(N_LEVELS):
            scores = scores + a["prods"][lv] * lmask_ref[1 + lv]
        v = src["rv"](rows, cols)
        state = state_ref[hd]
        o = _dot(a["q_in"], state.astype(BF16)) + _dot(scores.astype(BF16), v)
        column = jnp.broadcast_to(a["decay"], (SUBLANES, HGRN_HEAD_DIM)).T[:, :1]
        state_ref[hd] = state * column + _dot_tn(a["k_up"], v)
        y = _rms(o, og_ref[...]) * src["sg"](rows, cols).astype(F32)
        out_cols = slice(ATTN_WIDTH + hd * HGRN_HEAD_DIM, ATTN_WIDTH + (hd + 1) * HGRN_HEAD_DIM)
        o_ref[0, rows, out_cols] = y.astype(BF16)

    return (MIX_ROWS // CHUNK) * HGRN_HEADS, (products, finish)


def _mixer_kernel(x0_ref, xn_ref, g_ref, w_ref, qg_ref, kg_ref, lbp_ref, seg_ref, bias_ref, lsum_ref,
                  lmask_ref, og_ref, o_ref, h_ref, qt_ref, rq_ref, rk_ref, rv_ref, sg_ref, lf_ref,
                  kpad_ref, vt_ref, ones_ref, state_ref):
    seq = kpad_ref.shape[1] - PAD
    tiles = seq // MIX_ROWS
    s = pl.program_id(0)
    tile = lax.rem(s, tiles)
    nxt = jnp.minimum(s + 1, pl.num_programs(0) - 1)

    def projection(x_ref, step):
        slot = lax.rem(step, 2)
        par = lax.rem(lax.div(step, tiles), 2)
        rows = pl.ds(pl.multiple_of(PAD + lax.rem(step, tiles) * MIX_ROWS, MIX_ROWS), MIX_ROWS)

        def put(ref):
            def store(cols, val):
                ref[slot, :, cols] = val
            return store

        def put_k(cols, val):
            kpad_ref[par, rows, cols] = val

        def put_v(cols, val):
            vt_ref[par, cols, rows] = val.T

        def put_q(cols, val):
            qt_ref[slot, cols, :] = val.T

        dst = dict(q=put_q, k=put_k, v=put_v, rq=put(rq_ref), rk=put(rk_ref), rv=put(rv_ref),
                   sg=put(sg_ref), lf=put(lf_ref))
        return _projection_stages(x_ref, g_ref, w_ref, qg_ref, kg_ref, lbp_ref, seg_ref, h_ref, dst)

    @pl.when(s == 0)
    def _():
        kpad_ref[:, :PAD, :] = jnp.zeros((2, PAD, ATTN_WIDTH), BF16)
        vt_ref[:, :, :PAD] = jnp.zeros((2, ATTN_WIDTH, PAD), BF16)
        ones_ref[:, :PAD] = jnp.zeros((BF16_ROWS, PAD), BF16)
        ones_ref[:, PAD:] = jnp.ones((BF16_ROWS, seq), BF16)
        n_units, normalize, (project, tail) = projection(x0_ref, s)
        normalize()
        nxt_val = project(0)
        for i in range(n_units):
            cur, nxt_val = nxt_val, (project(i + 1) if i + 1 < n_units else None)
            tail(i, cur)

    @pl.when(tile == 0)
    def _():
        state_ref[...] = jnp.zeros_like(state_ref)

    slot = lax.rem(s, 2)
    par = lax.rem(lax.div(s, tiles), 2)
    tile_src = lambda ref: (lambda rows, cols: ref[slot, rows, cols])
    src = dict(qt=tile_src(qt_ref), rq=tile_src(rq_ref), rk=tile_src(rk_ref), rv=tile_src(rv_ref),
               sg=tile_src(sg_ref), lf=tile_src(lf_ref),
               k=lambda window, cols: kpad_ref[par, window, cols],
               vt=lambda cols, window: vt_ref[par, cols, window],
               ones=lambda window: ones_ref[:, window])
    n_proj, normalize, (proj_first, proj_second) = projection(xn_ref, nxt)
    n_attn, (attn_first, attn_second) = _attention_stages(tile, src, bias_ref, o_ref)
    n_hgrn, (hgrn_first, hgrn_second) = _hgrn_stages(src, lsum_ref, lmask_ref, og_ref, o_ref, state_ref)

    assert n_proj <= n_attn and n_hgrn % n_attn == 0
    per = n_hgrn // n_attn
    normalize()
    p_next, a_next, h_next = proj_first(0), attn_first(0), hgrn_first(0)
    for i in range(n_attn):
        a_cur, a_next = a_next, (attn_first(i + 1) if i + 1 < n_attn else None)
        for n in range(i * per, (i + 1) * per):
            h_cur, h_next = h_next, (hgrn_first(n + 1) if n + 1 < n_hgrn else None)
            hgrn_second(n, h_cur)
            if n == i * per:
                attn_second(i, a_cur)
                if i < n_proj:
                    p_cur, p_next = p_next, (proj_first(i + 1) if i + 1 < n_proj else None)
                    proj_second(i, p_cur)


def _mixer_call(x1, g, w_in, qg, kg, lbp, seg, bias, lsum, lmask, og):
    b, s, _ = x1.shape
    tiles = s // MIX_ROWS
    steps = b * tiles
    slot_pair = pltpu.VMEM((2, MIX_ROWS, HGRN_WIDTH), BF16)

    def next_tile(i):
        n = jnp.minimum(i + 1, steps - 1)
        return (n // tiles, n % tiles, 0)

    return pl.pallas_call(
        _mixer_kernel,
        grid=(steps,),
        in_specs=[pl.BlockSpec((1, MIX_ROWS, D_MODEL), lambda i: (0, 0, 0), pipeline_mode=pl.Buffered(1)),
                  pl.BlockSpec((1, MIX_ROWS, D_MODEL), next_tile),
                  _resident((1, D_MODEL)), _resident((D_MODEL, PROJ_COLS)),
                  _resident((1, ATTN_WIDTH)), _resident((1, ATTN_WIDTH)), _resident(lbp.shape),
                  _resident((MXU_TILE, MXU_TILE)), _resident(bias.shape), _resident(lsum.shape),
                  _resident(lmask.shape), _resident((1, HGRN_HEAD_DIM))],
        out_specs=pl.BlockSpec((1, MIX_ROWS, D_MODEL), lambda i: (i // tiles, i % tiles, 0)),
        out_shape=jax.ShapeDtypeStruct((b, s, D_MODEL), BF16),
        scratch_shapes=[pltpu.VMEM((MIX_ROWS, D_MODEL), BF16),
                        pltpu.VMEM((2, ATTN_WIDTH, MIX_ROWS), BF16)] + [slot_pair] * 5 + [
                        pltpu.VMEM((2, s + PAD, ATTN_WIDTH), BF16),
                        pltpu.VMEM((2, ATTN_WIDTH, s + PAD), BF16),
                        pltpu.VMEM((BF16_ROWS, s + PAD), BF16),
                        pltpu.VMEM((HGRN_HEADS, HGRN_HEAD_DIM, HGRN_HEAD_DIM), F32)],
        compiler_params=pltpu.CompilerParams(
            dimension_semantics=("arbitrary",), vmem_limit_bytes=VMEM_LIMIT),
        name="mixer",
    )(x1, x1, g, w_in, qg, kg, lbp, seg, bias, lsum, lmask, og)


def _bias_blocks():
    qq = np.arange(ATTN_TQ)[:, None]
    key = np.arange(ATTN_WIN)[None, :]
    near = PAD + qq - key < REL_CLIP
    band = (key // CHUNK >= qq // CHUNK) & (key // CHUNK <= qq // CHUNK + LEFT_CHUNKS)
    used = (near | ~band).any(axis=0).reshape(ATTN_WIN // ATTN_TQ, ATTN_TQ).any(axis=1)
    return tuple(int(j) for j in np.nonzero(used)[0])


BIAS_BLOCKS = _bias_blocks()


def _rel_bias_table(rel_bias):
    assert ATTN_TQ - 1 <= REL_CLIP
    rb = (rel_bias.astype(F32) - rel_bias.astype(F32)[:, 2 * REL_CLIP:]) * LOG2E
    qc = np.arange(ATTN_TQ)[:, None] // CHUNK
    kc = np.arange(ATTN_WIN)[None, :] // CHUNK
    band = (kc >= qc) & (kc <= qc + LEFT_CHUNKS)
    k0 = (PAD - REL_CLIP) // ATTN_TQ * ATTN_TQ
    w = ATTN_WIN - k0
    n = w + ATTN_TQ
    far = jnp.zeros((ATTN_HEADS, w - REL_CLIP), F32)
    near = rb[:, REL_CLIP - ATTN_TQ + 1:2 * REL_CLIP][:, ::-1]
    x = jnp.concatenate([far, near, jnp.zeros((ATTN_HEADS, 1), F32)], axis=1)
    rolled = jnp.tile(x, (1, ATTN_TQ))[:, :ATTN_TQ * (n - 1)].reshape(ATTN_HEADS, ATTN_TQ, n - 1)
    tbl = rolled[:, :, ATTN_TQ - 1:ATTN_TQ - 1 + w]
    tbl = jnp.where(band[:, k0:], tbl, -jnp.inf)
    tbl = tbl.reshape(ATTN_HEADS // 2, 2, ATTN_TQ, w).transpose(0, 3, 1, 2)
    tbl = tbl.reshape(ATTN_HEADS // 2, w // ATTN_TQ, ATTN_TQ, 2 * ATTN_TQ)
    blocks = []
    for j in BIAS_BLOCKS:
        if j * ATTN_TQ >= k0:
            blocks.append(tbl[:, j - k0 // ATTN_TQ])
        else:
            keys = slice(j * ATTN_TQ, (j + 1) * ATTN_TQ)
            mask = np.where(band[:, keys], 0.0, -np.inf).astype(np.float32).T
            blocks.append(jnp.broadcast_to(jnp.asarray(np.tile(mask, (1, 2))),
                                           (ATTN_HEADS // 2, ATTN_TQ, 2 * ATTN_TQ)))
    return jnp.stack(blocks, axis=1)


def kernel(x, ffn1_norm_g, ffn1_w_gate, ffn1_w_up, ffn1_w_down, mix_norm_g, w_in,
           attn_q_norm_g, attn_k_norm_g, attn_rel_bias, hgrn_lower_bounds, hgrn_out_norm_g,
           w_out, ffn2_norm_g, ffn2_w_gate, ffn2_w_up, ffn2_w_down):
    bsz, seq, _ = x.shape
    depth = ffn1_norm_g.shape[0]
    assert depth == 1 and seq % MIX_ROWS == 0 and (bsz * seq) % ROW_TILE == 0
    head_of_col = np.arange(MXU_TILE) // ATTN_HEAD_DIM
    seg = jnp.asarray(head_of_col[:, None] == head_of_col[None, :], dtype=BF16)
    lsum = jnp.asarray(_level_sum_matrix(), dtype=BF16)
    lmask = jnp.asarray(_level_masks())
    row = lambda g: g.reshape(1, -1).astype(F32)
    tile_heads = lambda g: jnp.tile(g.astype(F32), ATTN_HEADS).reshape(1, ATTN_WIDTH)

    xf = x.reshape(bsz * seq, D_MODEL)
    for l in range(depth):
        later = (w_in[l], w_out[l], ffn2_w_gate[l], ffn2_w_up[l], ffn2_w_down[l])
        x1, w_in_b, w_out_b, wg2_b, wu2_b, wd2_b = _ffn_call(
            xf, row(ffn1_norm_g[l]), ffn1_w_gate[l].astype(F32), ffn1_w_up[l].astype(F32),
            ffn1_w_down[l].astype(F32), cast=tuple(w.astype(F32) for w in later))
        mixed = _mixer_call(
            x1.reshape(bsz, seq, D_MODEL), row(mix_norm_g[l]), w_in_b,
            tile_heads(attn_q_norm_g[l]), tile_heads(attn_k_norm_g[l]), hgrn_lower_bounds.astype(F32),
            seg, _rel_bias_table(attn_rel_bias[l]), lsum, lmask, row(hgrn_out_norm_g[l]))
        xf, = _ffn_call(x1, row(ffn2_norm_g[l]), wg2_b, wu2_b, wd2_b,
                        mix=(mixed.reshape(bsz * seq, D_MODEL), w_out_b))
    return xf.reshape(bsz, seq, D_MODEL)
```
